```python
import jax, jax.numpy as jnp
from jax import lax
import numpy as np

D_MODEL = 1024
BATCH = 8
SEQ = 4096
DEPTH = 4

N_A = DEPTH // 2
N_B = DEPTH - N_A
PLE_DIM = 256
EPS = 1e-6
ROPE_THETA = 10000.0

A_HEADS = 4
A_DQK = 128
A_DV = 256
A_QK = A_HEADS * A_DQK
A_VW = A_HEADS * A_DV
A_CONV = 4
A_CHUNK = 64
A_COLS = 2 * A_QK + 3 * A_VW + 2 * A_HEADS

B_KV_HEADS = 4
B_HD = 128
DILATED_GROUPS = ((128, 1), (512, 4), (2048, 16))
B_GROUPS = len(DILATED_GROUPS)
B_QW = B_GROUPS * B_KV_HEADS * B_HD
B_VW = B_KV_HEADS * B_HD
B_COLS = B_QW + B_VW
Q_BLOCK = 128

kernel_name = "yoco_mlstm_dilated_swa_ple"


def rms_norm(x, g):
    xf = x.astype(jnp.float32)
    y = xf * lax.rsqrt(jnp.mean(xf * xf, axis=-1, keepdims=True) + EPS)
    return (y * g.astype(jnp.float32)).astype(x.dtype)


def rope_tables(seq, dim):
    inv = ROPE_THETA ** (-jnp.arange(0, dim, 2, dtype=jnp.float32) / dim)
    ang = jnp.arange(seq, dtype=jnp.float32)[:, None] * inv[None, :]
    return jnp.cos(ang), jnp.sin(ang)


def apply_rope(x, cos, sin):
    shape = (1, cos.shape[0]) + (1,) * (x.ndim - 3) + (cos.shape[1],)
    c = cos.reshape(shape)
    s = sin.reshape(shape)
    x1, x2 = jnp.split(x, 2, axis=-1)
    return jnp.concatenate([x1 * c - x2 * s, x2 * c + x1 * s], axis=-1)


def causal_depthwise_conv(x, w):
    K = w.shape[0]
    S = x.shape[1]
    xp = jnp.pad(x, ((0, 0), (K - 1, 0), (0, 0)))
    out = xp[:, 0:S] * w[0]
    for j in range(1, K):
        out = out + xp[:, j:j + S] * w[j]
    return out


def mlstm_chunkwise(q, k, v, i_pre, logf):
    Bsz, S, H, dk = q.shape
    dv = v.shape[-1]
    L = A_CHUNK
    nc = S // L

    def to_chunks(t):
        t = t.reshape((Bsz, nc, L, H) + t.shape[3:])
        return jnp.moveaxis(t, (1, 3), (0, 2))

    causal = jnp.tril(jnp.ones((L, L), dtype=bool))

    def step(carry, inp):
        C, n, m = carry
        qc, kc, vc, ic, fc = inp
        b = jnp.cumsum(fc, axis=-1)
        D = b[..., :, None] - b[..., None, :] + ic[..., None, :]
        D = jnp.where(causal, D, -jnp.inf)
        m_inter = b + m[..., None]
        m_s = jnp.maximum(m_inter, D.max(axis=-1))
        wts = jnp.exp(D - m_s[..., None]) * jnp.einsum('bhsd,bhjd->bhsj', qc, kc)
        inter = jnp.exp(m_inter - m_s)
        num = wts @ vc + inter[..., None] * jnp.einsum('bhsd,bhvd->bhsv', qc, C)
        den = wts.sum(axis=-1) + inter * jnp.einsum('bhsd,bhd->bhs', qc, n)
        h = num / jnp.maximum(jnp.abs(den), jnp.exp(-m_s))[..., None]
        g = b[..., -1:] - b + ic
        m_new = jnp.maximum(b[..., -1] + m, g.max(axis=-1))
        decay = jnp.exp(b[..., -1] + m - m_new)
        wj = jnp.exp(g - m_new[..., None])
        C = decay[..., None, None] * C + jnp.einsum('bhj,bhjv,bhjd->bhvd', wj, vc, kc)
        n = decay[..., None] * n + jnp.einsum('bhj,bhjd->bhd', wj, kc)
        return (C, n, m_new), h

    init = (jnp.zeros((Bsz, H, dv, dk), jnp.float32),
            jnp.zeros((Bsz, H, dk), jnp.float32),
            jnp.zeros((Bsz, H), jnp.float32))
    _, hs = lax.scan(step, init, (to_chunks(q), to_chunks(k), to_chunks(v),
                                  to_chunks(i_pre), to_chunks(logf)))
    return jnp.moveaxis(hs, (0, 2), (1, 3)).reshape(Bsz, S, H, dv)


def mlstm_layer(h, norm_g, w_in, conv_w, b_gate, hnorm_g, w_out):
    Bsz, S, _ = h.shape
    proj = rms_norm(h, norm_g) @ w_in
    qk, v, o, z, gates = jnp.split(
        proj, [2 * A_QK, 2 * A_QK + A_VW, 2 * A_QK + 2 * A_VW, 2 * A_QK + 3 * A_VW], axis=-1)
    qk = jax.nn.silu(causal_depthwise_conv(qk, conv_w)).astype(jnp.float32)
    q, k = jnp.split(qk, 2, axis=-1)
    q = q.reshape(Bsz, S, A_HEADS, A_DQK)
    k = k.reshape(Bsz, S, A_HEADS, A_DQK) * (A_DQK ** -0.5)
    v = v.astype(jnp.float32).reshape(Bsz, S, A_HEADS, A_DV)
    gates = gates.astype(jnp.float32) + b_gate.astype(jnp.float32)
    i_pre, f_pre = jnp.split(gates, 2, axis=-1)
    ht = mlstm_chunkwise(q, k, v, i_pre, jax.nn.log_sigmoid(f_pre))
    ht = jax.nn.sigmoid(o.astype(jnp.float32)).reshape(Bsz, S, A_HEADS, A_DV) * ht
    ht = rms_norm(ht, hnorm_g.reshape(A_HEADS, A_DV))
    y = ht.reshape(Bsz, S, A_VW) * jax.nn.silu(z.astype(jnp.float32))
    return y.astype(h.dtype) @ w_out


def dilated_window_attention(q, k, v, window, dilation):
    Bsz, S, H, hd = q.shape
    r = dilation
    n_back = window // dilation
    s_sub = S // r
    nb = -(-s_sub // Q_BLOCK)
    P = nb * Q_BLOCK

    def to_blocks(t):
        t = t.reshape(Bsz, s_sub, r, H, hd).transpose(0, 2, 3, 1, 4)
        t = jnp.pad(t, ((0, 0), (0, 0), (0, 0), (0, P - s_sub), (0, 0)))
        return t.reshape(Bsz, r, H, nb, Q_BLOCK, hd)

    def with_prev(t):
        prev = jnp.pad(t, ((0, 0), (0, 0), (0, 0), (1, 0), (0, 0), (0, 0)))[:, :, :, :-1]
        return jnp.concatenate([prev, t], axis=4)

    qb = to_blocks(q)
    kw = with_prev(to_blocks(k))
    vw = with_prev(to_blocks(v))
    s = jnp.einsum('brhnqd,brhnkd->brhnqk', qb, kw) * (hd ** -0.5)
    blk = jnp.arange(nb)[:, None, None]
    qi = jnp.arange(Q_BLOCK)[None, :, None]
    kj = jnp.arange(2 * Q_BLOCK)[None, None, :]
    dist = Q_BLOCK + qi - kj
    kpos = (blk - 1) * Q_BLOCK + kj
    mask = (dist >= 0) & (dist <= n_back) & (kpos >= 0)
    s = jnp.where(mask, s, -jnp.inf)
    mx = s.max(axis=-1, keepdims=True)
    e = jnp.exp(s - mx)
    den = e.sum(axis=-1)
    o = jnp.einsum('brhnqk,brhnkd->brhnqd', e, vw) / den[..., None]
    lse = mx[..., 0] + jnp.log(den)
    o = o.reshape(Bsz, r, H, P, hd)[:, :, :, :s_sub].transpose(0, 3, 1, 2, 4).reshape(Bsz, S, H, hd)
    lse = lse.reshape(Bsz, r, H, P)[..., :s_sub].transpose(0, 3, 1, 2).reshape(Bsz, S, H)
    return o, lse


def shared_kv(h, norm_g, w_kv, knorm_g, cos, sin):
    Bsz, S, _ = h.shape
    kv = (rms_norm(h, norm_g) @ w_kv).astype(jnp.float32)
    k, v = jnp.split(kv, 2, axis=-1)
    k = k.reshape(Bsz, S, B_KV_HEADS, B_HD)
    v = v.reshape(Bsz, S, B_KV_HEADS, B_HD)
    k = apply_rope(rms_norm(k, knorm_g), cos, sin)
    return k, v


def dilated_layer(h, k, v, norm_g, w_in, qnorm_g, w_out, cos, sin):
    Bsz, S, _ = h.shape
    proj = rms_norm(h, norm_g) @ w_in
    q, z = jnp.split(proj, [B_QW], axis=-1)
    q = q.astype(jnp.float32).reshape(Bsz, S, B_GROUPS, B_KV_HEADS, B_HD)
    q = apply_rope(rms_norm(q, qnorm_g), cos, sin)
    outs, lses = [], []
    for g, (window, dilation) in enumerate(DILATED_GROUPS):
        o_g, l_g = dilated_window_attention(q[:, :, g], k, v, window, dilation)
        outs.append(o_g)
        lses.append(l_g)
    wts = jax.nn.softmax(jnp.stack(lses, axis=0), axis=0)
    o = jnp.einsum('gbsh,gbshd->bshd', wts, jnp.stack(outs, axis=0))
    y = o.reshape(Bsz, S, B_VW) * jax.nn.silu(z.astype(jnp.float32))
    return y.astype(h.dtype) @ w_out


def ple_add(h, p_l, norm_g, w_gate, w_proj):
    gate = jax.nn.sigmoid((rms_norm(h, norm_g) @ w_gate).astype(jnp.float32))
    return h + ((p_l @ w_proj).astype(jnp.float32) * gate).astype(h.dtype)


def setup_inputs(seed: int = 0) -> dict:
    key = jax.random.key(seed)
    ks = jax.random.split(key, 24)
    nrm = jax.random.normal
    f32 = jnp.float32
    x = nrm(ks[0], (BATCH, SEQ, D_MODEL), f32)
    p = nrm(ks[1], (DEPTH, BATCH, SEQ, PLE_DIM), f32)
    norm_a = 1.0 + 0.05 * nrm(ks[2], (N_A, D_MODEL), f32)
    w_in_a = nrm(ks[3], (N_A, D_MODEL, A_COLS), f32) * D_MODEL ** -0.5
    conv_a = nrm(ks[4], (N_A, A_CONV, 2 * A_QK), f32) * A_CONV ** -0.5
    b_i = 0.1 * nrm(ks[5], (N_A, A_HEADS), f32)
    b_f = jnp.linspace(3.0, 6.0, A_HEADS, dtype=f32)[None, :] + 0.1 * nrm(ks[6], (N_A, A_HEADS), f32)
    b_gate_a = jnp.concatenate([b_i, b_f], axis=-1)
    hnorm_a = 1.0 + 0.05 * nrm(ks[7], (N_A, A_VW), f32)
    w_out_a = nrm(ks[8], (N_A, A_VW, D_MODEL), f32) * A_VW ** -0.5
    norm_kv = 1.0 + 0.05 * nrm(ks[9], (D_MODEL,), f32)
    w_kv = nrm(ks[10], (D_MODEL, 2 * B_VW), f32) * D_MODEL ** -0.5
    knorm = 1.0 + 0.05 * nrm(ks[11], (B_HD,), f32)
    norm_b = 1.0 + 0.05 * nrm(ks[12], (N_B, D_MODEL), f32)
    w_in_b = nrm(ks[13], (N_B, D_MODEL, B_COLS), f32) * D_MODEL ** -0.5
    qnorm_b = 1.0 + 0.05 * nrm(ks[14], (N_B, B_HD), f32)
    w_out_b = nrm(ks[15], (N_B, B_VW, D_MODEL), f32) * B_VW ** -0.5
    ple_norm = 1.0 + 0.05 * nrm(ks[16], (DEPTH, D_MODEL), f32)
    w_ple_gate = nrm(ks[17], (DEPTH, D_MODEL, D_MODEL), f32) * D_MODEL ** -0.5
    w_ple = nrm(ks[18], (DEPTH, PLE_DIM, D_MODEL), f32) * PLE_DIM ** -0.5
    return {"x": x, "p": p, "norm_a": norm_a, "w_in_a": w_in_a, "conv_a": conv_a,
            "b_gate_a": b_gate_a, "hnorm_a": hnorm_a, "w_out_a": w_out_a,
            "norm_kv": norm_kv, "w_kv": w_kv, "knorm": knorm,
            "norm_b": norm_b, "w_in_b": w_in_b, "qnorm_b": qnorm_b, "w_out_b": w_out_b,
            "ple_norm": ple_norm, "w_ple_gate": w_ple_gate, "w_ple": w_ple}


def reference(x, p, norm_a, w_in_a, conv_a, b_gate_a, hnorm_a, w_out_a,
              norm_kv, w_kv, knorm, norm_b, w_in_b, qnorm_b, w_out_b,
              ple_norm, w_ple_gate, w_ple):
    S = x.shape[1]
    cos, sin = rope_tables(S, B_HD)
    h = x
    k_sh = None
    v_sh = None
    for layer in range(DEPTH):
        if layer < N_A:
            h = h + mlstm_layer(h, norm_a[layer], w_in_a[layer], conv_a[layer],
                                b_gate_a[layer], hnorm_a[layer], w_out_a[layer])
        else:
            if layer == N_A:
                k_sh, v_sh = shared_kv(h, norm_kv, w_kv, knorm, cos, sin)
            j = layer - N_A
            h = h + dilated_layer(h, k_sh, v_sh, norm_b[j], w_in_b[j], qnorm_b[j],
                                  w_out_b[j], cos, sin)
        h = ple_add(h, p[layer], ple_norm[layer], w_ple_gate[layer], w_ple[layer])
    return h
```

```python
import functools

import jax
import jax.numpy as jnp
from jax import lax
from jax.experimental import pallas as pl
from jax.experimental.pallas import tpu as pltpu

F32 = jnp.float32
BF16 = jnp.bfloat16

EPS = 1e-6
ROPE_THETA = 10000.0

A_HEADS = 4
A_DQK = 128
A_DV = 256
A_QK = A_HEADS * A_DQK
A_VW = A_HEADS * A_DV
A_CONV = 4
MLSTM_CHUNK = 256

B_KV_HEADS = 4
B_HD = 128
B_VW = B_KV_HEADS * B_HD
DILATED_GROUPS = ((128, 1), (512, 4), (2048, 16))
N_BACK = 128
MAX_DIL = 16
ATT_SUB = 128

LANES = 128
VMEM_LIMIT = 56 * 1024 * 1024


def _dot(a, b):
    return jnp.dot(a, b, preferred_element_type=F32)


def _dot_nt(a, b):
    return lax.dot_general(a, b, (((1,), (1,)), ((), ())), preferred_element_type=F32)


def _rms(x, g):
    ms = jnp.mean(x * x, axis=-1, keepdims=True)
    return x * lax.rsqrt(ms + EPS) * g


def _sigmoid(x):
    return 1.0 / (1.0 + jnp.exp(-x))


def _silu(x):
    return x * _sigmoid(x)


def _log_sigmoid(x):
    return jnp.minimum(x, 0.0) - jnp.log1p(jnp.exp(-jnp.abs(x)))


def _const_spec(shape):
    nd = len(shape)
    return pl.BlockSpec(shape, lambda *_: (0,) * nd, pipeline_mode=pl.Buffered(1))


def _mlstm_body(h_ref, p_ref, ng_ref, wqk_ref, wv_ref, wo_ref, wz_ref, wg_ref, conv_ref, bg_ref,
                hng_ref, wout_ref, png_ref, wpg_ref, wpp_ref, out_ref,
                qk_scr, ct_scr, n_scr, m_scr):
    L = h_ref.shape[0]
    step = pl.program_id(1)

    @pl.when(step == 0)
    def _():
        qk_scr[0:8, :] = jnp.zeros((8, 2 * A_QK), F32)
        ct_scr[...] = jnp.zeros(ct_scr.shape, F32)
        n_scr[...] = jnp.zeros(n_scr.shape, F32)
        m_scr[...] = jnp.zeros(m_scr.shape, F32)

    h = h_ref[...]
    xn = _rms(h, ng_ref[...]).astype(BF16)
    qk_pre = _dot(xn, wqk_ref[...])
    v = _dot(xn, wv_ref[...])
    o_pre = _dot(xn, wo_ref[...])
    z = _dot(xn, wz_ref[...])
    g2 = _dot(xn, wg_ref[...]) + bg_ref[...]
    gi = g2[:, :LANES]
    gf = g2[:, LANES:]

    qk_scr[8:8 + L, :] = qk_pre
    cw = conv_ref[...]
    conv = (qk_scr[5:5 + L, :] * cw[0:1, :] + qk_scr[6:6 + L, :] * cw[1:2, :]
            + qk_scr[7:7 + L, :] * cw[2:3, :] + qk_pre * cw[3:4, :])
    qk_scr[0:8, :] = qk_scr[L:L + 8, :]
    qk = _silu(conv)
    q = qk[:, :A_QK]
    k = qk[:, A_QK:] * (A_DQK ** -0.5)

    row_i = lax.broadcasted_iota(jnp.int32, (L, L), 0)
    col_i = lax.broadcasted_iota(jnp.int32, (L, L), 1)
    causal = col_i <= row_i
    tril = jnp.where(causal, 1.0, 0.0).astype(BF16)
    lf = _log_sigmoid(gf)
    lf1 = lf.astype(BF16)
    r1 = lf - lf1.astype(F32)
    lf2 = r1.astype(BF16)
    lf3 = (r1 - lf2.astype(F32)).astype(BF16)
    b_col = _dot(tril, lf1) + _dot(tril, lf2) + _dot(tril, lf3)
    bb_col = gi - b_col
    bb_row = bb_col.T[0:8, :]
    lane8 = lax.broadcasted_iota(jnp.int32, (8, L), 1)
    cm = bb_row
    s = 1
    while s < L:
        cm = jnp.maximum(cm, jnp.where(lane8 >= s, pltpu.roll(cm, s, axis=1), -jnp.inf))
        s *= 2
    m_old = m_scr[...]
    mm_row = jnp.maximum(cm, m_old[:, 0:1])
    mm_col = jnp.concatenate([mm_row, jnp.zeros((LANES - 8, L), F32)], axis=0).T

    outs = []
    for hd in range(A_HEADS):
        qh = q[:, hd * A_DQK:(hd + 1) * A_DQK]
        kh = k[:, hd * A_DQK:(hd + 1) * A_DQK]
        vh = v[:, hd * A_DV:(hd + 1) * A_DV]
        qb = qh.astype(BF16)
        kb = kh.astype(BF16)
        mc = mm_col[:, hd:hd + 1]
        br = bb_row[hd:hd + 1, :]
        m_h = m_old[hd:hd + 1, 0:1]
        m_last = mm_row[hd:hd + 1, L - 1:L]
        b_last = b_col[L - 1:L, hd:hd + 1]

        sc = _dot_nt(qb, kb)
        w = jnp.where(causal, jnp.exp(br - mc), 0.0) * sc
        den = jnp.sum(w, axis=1, keepdims=True)
        num = _dot(w.astype(BF16), vh.astype(BF16))
        inter = jnp.exp(m_h - mc)
        ct = ct_scr[hd]
        nrow = n_scr[hd:hd + 1, :]
        num = num + inter * _dot(qb, ct.astype(BF16))
        den = den + inter * jnp.sum(qh * nrow, axis=1, keepdims=True)
        ms = b_col[:, hd:hd + 1] + mc
        outs.append(num / jnp.maximum(jnp.abs(den), jnp.exp(-ms)))

        wj = jnp.exp(bb_col[:, hd:hd + 1] - m_last)
        decay = jnp.exp(m_h - m_last)
        ct_scr[hd] = decay * ct + _dot(kh.T.astype(BF16), (wj * vh).astype(BF16))
        n_scr[hd:hd + 1, :] = decay * nrow + jnp.sum(wj * kh, axis=0, keepdims=True)
        m_scr[hd:hd + 1, :] = jnp.broadcast_to(b_last + m_last, (1, LANES))

    hng = hng_ref[...]
    og = _sigmoid(o_pre)
    ys = []
    for hd in range(A_HEADS):
        sl = slice(hd * A_DV, (hd + 1) * A_DV)
        ht = og[:, sl] * outs[hd]
        ys.append(_rms(ht, hng[:, sl]) * _silu(z[:, sl]))
    y = jnp.concatenate(ys, axis=1).astype(BF16)
    hn = h + _dot(y, wout_ref[...])

    gate = _sigmoid(_dot(_rms(hn, png_ref[...]).astype(BF16), wpg_ref[...]))
    emb = _dot(p_ref[...].astype(BF16), wpp_ref[...])
    out_ref[...] = hn + emb * gate


def _mlstm_layer(h, p, layer, norm_g, w_in, conv_w, b_gate, hnorm_g, w_out, ple_norm, w_pg, w_pp):
    B, S, D = h.shape
    L = MLSTM_CHUNK
    assert S % L == 0 and L % 8 == 0
    P = p.shape[-1]
    wb = w_in.astype(BF16)
    o0 = 2 * A_QK
    wqk, wv, wo, wz = wb[:, :o0], wb[:, o0:o0 + A_VW], wb[:, o0 + A_VW:o0 + 2 * A_VW], wb[:, o0 + 2 * A_VW:o0 + 3 * A_VW]
    wgi = wb[:, o0 + 3 * A_VW:o0 + 3 * A_VW + A_HEADS]
    wgf = wb[:, o0 + 3 * A_VW + A_HEADS:]
    pad = jnp.zeros((D, LANES - A_HEADS), BF16)
    wg = jnp.concatenate([wgi, pad, wgf, pad], axis=1)
    bpad = jnp.zeros((LANES - A_HEADS,), F32)
    bg = jnp.concatenate([b_gate[:A_HEADS], bpad, b_gate[A_HEADS:], bpad])[None, :]

    row = lambda b, i: (b, i, 0)
    in_specs = [
        pl.BlockSpec((None, L, D), row),
        pl.BlockSpec((None, None, L, P), lambda b, i: (layer, b, i, 0)),
        _const_spec((1, D)),
        _const_spec((D, 2 * A_QK)), _const_spec((D, A_VW)), _const_spec((D, A_VW)), _const_spec((D, A_VW)),
        _const_spec((D, 2 * LANES)),
        _const_spec((A_CONV, 2 * A_QK)), _const_spec((1, 2 * LANES)), _const_spec((1, A_VW)),
        _const_spec((A_VW, D)), _const_spec((1, D)), _const_spec((D, D)), _const_spec((P, D)),
    ]
    return pl.pallas_call(
        _mlstm_body,
        grid=(B, S // L),
        in_specs=in_specs,
        out_specs=pl.BlockSpec((None, L, D), row),
        out_shape=jax.ShapeDtypeStruct((B, S, D), F32),
        scratch_shapes=[
            pltpu.VMEM((L + 8, 2 * A_QK), F32),
            pltpu.VMEM((A_HEADS, A_DQK, A_DV), F32),
            pltpu.VMEM((8, A_DQK), F32),
            pltpu.VMEM((8, LANES), F32),
        ],
        compiler_params=pltpu.CompilerParams(
            dimension_semantics=("arbitrary", "arbitrary"), vmem_limit_bytes=VMEM_LIMIT),
        name="mlstm_layer",
    )(h, p, norm_g[None, :], wqk, wv, wo, wz, wg, conv_w, bg, hnorm_g[None, :],
      w_out.astype(BF16), ple_norm[None, :], w_pg.astype(BF16), w_pp.astype(BF16))


def _norm_rope(x, g, cos2, sin2):
    xn = _rms(x, g)
    return xn * cos2 + pltpu.roll(xn, B_HD // 2, axis=1) * sin2


def _qkv_body(*refs, with_kv):
    if with_kv:
        (h_ref, cos_ref, sin_ref, ng_ref, win_ref, qg_ref, nkv_ref, wkv_ref, kg_ref,
         q0_ref, q1_ref, q2_ref, z_ref, k0_ref, k1_ref, k2_ref, v0_ref, v1_ref, v2_ref) = refs
    else:
        (h_ref, cos_ref, sin_ref, ng_ref, win_ref, qg_ref, q0_ref, q1_ref, q2_ref, z_ref) = refs
    h = h_ref[...]
    cos2 = cos_ref[...]
    sin2 = sin_ref[...]
    xn = _rms(h, ng_ref[...]).astype(BF16)
    proj = _dot(xn, win_ref[...])
    qg = qg_ref[...]
    for g, q_ref in enumerate((q0_ref, q1_ref, q2_ref)):
        for hd in range(B_KV_HEADS):
            c0 = (g * B_KV_HEADS + hd) * B_HD
            q_ref[:, hd * B_HD:(hd + 1) * B_HD] = _norm_rope(proj[:, c0:c0 + B_HD], qg, cos2, sin2).astype(BF16)
    z_ref[...] = proj[:, 3 * B_VW:]
    if with_kv:
        xk = _rms(h, nkv_ref[...]).astype(BF16)
        kv = _dot(xk, wkv_ref[...])
        kg = kg_ref[...]
        kparts = [_norm_rope(kv[:, hd * B_HD:(hd + 1) * B_HD], kg, cos2, sin2) for hd in range(B_KV_HEADS)]
        kk = jnp.concatenate(kparts, axis=1).astype(BF16)
        vv = kv[:, B_VW:].astype(BF16)
        for r in (k0_ref, k1_ref, k2_ref):
            r[...] = kk
        for r in (v0_ref, v1_ref, v2_ref):
            r[...] = vv


def _qkv_proj(h, cos2, sin2, norm_g, w_in, qnorm_g, kv_params=None):
    B, S, D = h.shape
    R = MAX_DIL
    n = S // R
    with_kv = kv_params is not None
    hv = h.reshape(B, n, R * D)
    cosv = cos2.reshape(n, R * B_HD)
    sinv = sin2.reshape(n, R * B_HD)

    def lay_specs(width, dtype):
        shapes = [
            jax.ShapeDtypeStruct((B, n, R * width), dtype),
            jax.ShapeDtypeStruct((B, 4, n, 4 * width), dtype),
            jax.ShapeDtypeStruct((B, R, n, width), dtype),
        ]
        specs = [
            pl.BlockSpec((None, n, width), lambda b, c: (b, 0, c)),
            pl.BlockSpec((None, None, n, width), lambda b, c: (b, c % 4, 0, c // 4)),
            pl.BlockSpec((None, None, n, width), lambda b, c: (b, c, 0, 0)),
        ]
        return shapes, specs

    q_shapes, q_specs = lay_specs(B_VW, BF16)
    out_shape = q_shapes + [jax.ShapeDtypeStruct((B, n, R * B_VW), F32)]
    out_specs = q_specs + [pl.BlockSpec((None, n, B_VW), lambda b, c: (b, 0, c))]
    in_specs = [
        pl.BlockSpec((None, n, D), lambda b, c: (b, 0, c)),
        pl.BlockSpec((n, B_HD), lambda b, c: (0, c)),
        pl.BlockSpec((n, B_HD), lambda b, c: (0, c)),
        _const_spec((1, D)), _const_spec((D, 4 * B_VW)), _const_spec((1, B_HD)),
    ]
    args = [hv, cosv, sinv, norm_g[None, :], w_in.astype(BF16), qnorm_g[None, :]]
    if with_kv:
        norm_kv, w_kv, knorm = kv_params
        in_specs += [_const_spec((1, D)), _const_spec((D, 2 * B_VW)), _const_spec((1, B_HD))]
        args += [norm_kv[None, :], w_kv.astype(BF16), knorm[None, :]]
        for _ in range(2):
            shp, sp = lay_specs(B_VW, BF16)
            out_shape += shp
            out_specs += sp
    outs = pl.pallas_call(
        functools.partial(_qkv_body, with_kv=with_kv),
        grid=(B, R),
        in_specs=in_specs,
        out_specs=out_specs,
        out_shape=out_shape,
        compiler_params=pltpu.CompilerParams(
            dimension_semantics=("arbitrary", "arbitrary"), vmem_limit_bytes=VMEM_LIMIT),
        name="qkv_proj_kv" if with_kv else "qkv_proj",
    )(*args)

    def views(a0, a1, a2):
        return (a0.reshape(B, 1, S, B_VW), a1.reshape(B, 4, S // 4, B_VW), a2)

    q = views(*outs[0:3])
    z = outs[3].reshape(B, S, B_VW)
    if with_kv:
        return q, z, views(*outs[4:7]), views(*outs[7:10])
    return q, z


def _attn_body(q_ref, kh_ref, kc_ref, vh_ref, vc_ref, o_ref, lse_ref):
    qblk = q_ref.shape[0]
    u0 = pl.program_id(2) * qblk
    q = q_ref[...]
    k = jnp.concatenate([kh_ref[...], kc_ref[...]], axis=0)
    v = jnp.concatenate([vh_ref[...], vc_ref[...]], axis=0)
    nk = ATT_SUB + N_BACK
    qq = lax.broadcasted_iota(jnp.int32, (ATT_SUB, nk), 0)
    kk = lax.broadcasted_iota(jnp.int32, (ATT_SUB, nk), 1)
    dist = N_BACK + qq - kk
    band = (dist >= 0) & (dist <= N_BACK)
    lane = lax.broadcasted_iota(jnp.int32, (ATT_SUB, LANES), 1)
    scale = B_HD ** -0.5
    for sb in range(qblk // ATT_SUB):
        r0 = sb * ATT_SUB
        kpos = u0 + (r0 - N_BACK) + kk
        bias = jnp.where(band & (kpos >= 0), 0.0, -jnp.inf)
        lse_blk = jnp.zeros((ATT_SUB, LANES), F32)
        for hd in range(B_KV_HEADS):
            cs = slice(hd * B_HD, (hd + 1) * B_HD)
            s = _dot_nt(q[r0:r0 + ATT_SUB, cs], k[r0:r0 + nk, cs]) * scale + bias
            mx = jnp.max(s, axis=1, keepdims=True)
            e = jnp.exp(s - mx)
            den = jnp.sum(e, axis=1, keepdims=True)
            o = _dot(e.astype(BF16), v[r0:r0 + nk, cs]) / den
            o_ref[r0:r0 + ATT_SUB, cs] = o.astype(BF16)
            lse_blk = jnp.where(lane == hd, mx + jnp.log(den), lse_blk)
        lse_ref[r0:r0 + ATT_SUB, :] = lse_blk


def _window_attn(q, k, v):
    B, r, N, W = q.shape
    qblk = min(N, 512)
    per = qblk // N_BACK
    cur = lambda b, c, i: (b, c, i, 0)
    halo = lambda b, c, i: (b, c, jnp.maximum(i * per - 1, 0), 0)
    o, lse = pl.pallas_call(
        _attn_body,
        grid=(B, r, N // qblk),
        in_specs=[
            pl.BlockSpec((None, None, qblk, W), cur),
            pl.BlockSpec((None, None, N_BACK, W), halo),
            pl.BlockSpec((None, None, qblk, W), cur),
            pl.BlockSpec((None, None, N_BACK, W), halo),
            pl.BlockSpec((None, None, qblk, W), cur),
        ],
        out_specs=[
            pl.BlockSpec((None, qblk, W), lambda b, c, i: (b, i, c)),
            pl.BlockSpec((None, qblk, LANES), lambda b, c, i: (b, i, c)),
        ],
        out_shape=[
            jax.ShapeDtypeStruct((B, N, r * W), BF16),
            jax.ShapeDtypeStruct((B, N, r * LANES), F32),
        ],
        compiler_params=pltpu.CompilerParams(
            dimension_semantics=("arbitrary", "arbitrary", "arbitrary"), vmem_limit_bytes=VMEM_LIMIT),
        name=f"window_attn_r{r}",
    )(q, k, k, v, v)
    return o.reshape(B, N * r, W), lse.reshape(B, N * r, LANES)


def _post_body(h_ref, p_ref, z_ref, o0_ref, o1_ref, o2_ref, l0_ref, l1_ref, l2_ref,
               wout_ref, png_ref, wpg_ref, wpp_ref, out_ref):
    l0, l1, l2 = l0_ref[...], l1_ref[...], l2_ref[...]
    mx = jnp.maximum(jnp.maximum(l0, l1), l2)
    e0, e1, e2 = jnp.exp(l0 - mx), jnp.exp(l1 - mx), jnp.exp(l2 - mx)
    tot = e0 + e1 + e2
    w0, w1, w2 = e0 / tot, e1 / tot, e2 / tot
    z = z_ref[...]
    ys = []
    for hd in range(B_KV_HEADS):
        cs = slice(hd * B_HD, (hd + 1) * B_HD)
        o = (w0[:, hd:hd + 1] * o0_ref[:, cs].astype(F32) + w1[:, hd:hd + 1] * o1_ref[:, cs].astype(F32)
             + w2[:, hd:hd + 1] * o2_ref[:, cs].astype(F32))
        ys.append(o * _silu(z[:, cs]))
    y = jnp.concatenate(ys, axis=1).astype(BF16)
    hn = h_ref[...] + _dot(y, wout_ref[...])
    gate = _sigmoid(_dot(_rms(hn, png_ref[...]).astype(BF16), wpg_ref[...]))
    emb = _dot(p_ref[...].astype(BF16), wpp_ref[...])
    out_ref[...] = hn + emb * gate


def _attn_post(h, p, layer, z, os_, lses, w_out, ple_norm, w_pg, w_pp):
    B, S, D = h.shape
    P = p.shape[-1]
    n = 512
    assert S % n == 0
    row = lambda b, i: (b, i, 0)
    in_specs = [
        pl.BlockSpec((None, n, D), row),
        pl.BlockSpec((None, None, n, P), lambda b, i: (layer, b, i, 0)),
        pl.BlockSpec((None, n, B_VW), row),
        pl.BlockSpec((None, n, B_VW), row), pl.BlockSpec((None, n, B_VW), row), pl.BlockSpec((None, n, B_VW), row),
        pl.BlockSpec((None, n, LANES), row), pl.BlockSpec((None, n, LANES), row), pl.BlockSpec((None, n, LANES), row),
        _const_spec((B_VW, D)), _const_spec((1, D)), _const_spec((D, D)), _const_spec((P, D)),
    ]
    return pl.pallas_call(
        _post_body,
        grid=(B, S // n),
        in_specs=in_specs,
        out_specs=pl.BlockSpec((None, n, D), row),
        out_shape=jax.ShapeDtypeStruct((B, S, D), F32),
        compiler_params=pltpu.CompilerParams(
            dimension_semantics=("arbitrary", "arbitrary"), vmem_limit_bytes=VMEM_LIMIT),
        name="attn_post",
    )(h, p, z, *os_, *lses, w_out.astype(BF16), ple_norm[None, :], w_pg.astype(BF16), w_pp.astype(BF16))


def _rope_tables(seq):
    inv = ROPE_THETA ** (-jnp.arange(0, B_HD, 2, dtype=F32) / B_HD)
    ang = jnp.arange(seq, dtype=F32)[:, None] * inv[None, :]
    cos, sin = jnp.cos(ang), jnp.sin(ang)
    return jnp.concatenate([cos, cos], axis=1), jnp.concatenate([-sin, sin], axis=1)


def kernel(x, p, norm_a, w_in_a, conv_a, b_gate_a, hnorm_a, w_out_a, norm_kv, w_kv, knorm, norm_b, w_in_b, qnorm_b, w_out_b, ple_norm, w_ple_gate, w_ple):
    S = x.shape[1]
    n_a = norm_a.shape[0]
    n_b = norm_b.shape[0]
    assert all(w // d == N_BACK for w, d in DILATED_GROUPS)
    cos2, sin2 = _rope_tables(S)
    h = x
    for l in range(n_a):
        h = _mlstm_layer(h, p, l, norm_a[l], w_in_a[l], conv_a[l], b_gate_a[l], hnorm_a[l], w_out_a[l],
                         ple_norm[l], w_ple_gate[l], w_ple[l])
    ks = vs = None
    for j in range(n_b):
        l = n_a + j
        if j == 0:
            qs, z, ks, vs = _qkv_proj(h, cos2, sin2, norm_b[j], w_in_b[j], qnorm_b[j], (norm_kv, w_kv, knorm))
        else:
            qs, z = _qkv_proj(h, cos2, sin2, norm_b[j], w_in_b[j], qnorm_b[j])
        res = [_window_attn(qs[g], ks[g], vs[g]) for g in range(len(DILATED_GROUPS))]
        h = _attn_post(h, p, l, z, [r[0] for r in res], [r[1] for r in res],
                       w_out_b[j], ple_norm[l], w_ple_gate[l], w_ple[l])
    return h
```

```python
import functools

import jax
import jax.numpy as jnp
from jax import lax
from jax.experimental import pallas as pl
from jax.experimental.pallas import tpu as pltpu

F32 = jnp.float32
BF16 = jnp.bfloat16

EPS = 1e-6
ROPE_THETA = 10000.0

A_HEADS = 4
A_DQK = 128
A_DV = 256
A_QK = A_HEADS * A_DQK
A_VW = A_HEADS * A_DV
A_CONV = 4
MLSTM_CHUNK = 256

B_KV_HEADS = 4
B_HD = 128
B_VW = B_KV_HEADS * B_HD
DILATED_GROUPS = ((128, 1), (512, 4), (2048, 16))
N_BACK = 128
MAX_DIL = 16
ATT_SUB = 128

LANES = 128
VMEM_LIMIT = 56 * 1024 * 1024


def _dot(a, b):
    return jnp.dot(a, b, preferred_element_type=F32)


def _dot_nt(a, b):
    return lax.dot_general(a, b, (((1,), (1,)), ((), ())), preferred_element_type=F32)


def _rms(x, g):
    ms = jnp.mean(x * x, axis=-1, keepdims=True)
    return x * lax.rsqrt(ms + EPS) * g


def _sigmoid(x):
    return 1.0 / (1.0 + jnp.exp(-x))


def _silu(x):
    return x * _sigmoid(x)


def _log_sigmoid(x):
    return jnp.minimum(x, 0.0) - jnp.log1p(jnp.exp(-jnp.abs(x)))


def _const_spec(shape):
    nd = len(shape)
    return pl.BlockSpec(shape, lambda *_: (0,) * nd, pipeline_mode=pl.Buffered(1))


def _mlstm_body(h_ref, p_ref, ng_ref, wqk_ref, wv_ref, wo_ref, wz_ref, wg_ref, conv_ref, bg_ref,
                hng_ref, wout_ref, png_ref, wpg_ref, wpp_ref, out_ref,
                qk_scr, ct_scr, n_scr, m_scr):
    L = h_ref.shape[0]
    step = pl.program_id(1)

    @pl.when(step == 0)
    def _():
        qk_scr[0:8, :] = jnp.zeros((8, 2 * A_QK), F32)
        ct_scr[...] = jnp.zeros(ct_scr.shape, F32)
        n_scr[...] = jnp.zeros(n_scr.shape, F32)
        m_scr[...] = jnp.zeros(m_scr.shape, F32)

    h = h_ref[...]
    xn = _rms(h, ng_ref[...]).astype(BF16)
    g2 = _dot(xn, wg_ref[...]) + bg_ref[...]
    gi = g2[:, :LANES]
    gf = g2[:, LANES:]

    row_i = lax.broadcasted_iota(jnp.int32, (L, L), 0)
    col_i = lax.broadcasted_iota(jnp.int32, (L, L), 1)
    causal = col_i <= row_i
    tril = jnp.where(causal, 1.0, 0.0).astype(BF16)
    lf = _log_sigmoid(gf)
    lf1 = lf.astype(BF16)
    r1 = lf - lf1.astype(F32)
    lf2 = r1.astype(BF16)
    lf3 = (r1 - lf2.astype(F32)).astype(BF16)
    b_col = _dot(tril, lf1) + _dot(tril, lf2) + _dot(tril, lf3)
    bb_col = gi - b_col
    bb_row = bb_col.T[0:8, :]
    m_old = m_scr[...]

    qk_pre = _dot(xn, wqk_ref[...])
    v = _dot(xn, wv_ref[...])
    o_pre = _dot(xn, wo_ref[...])
    z = _dot(xn, wz_ref[...])

    qk_scr[8:8 + L, :] = qk_pre
    cw = conv_ref[...]
    conv = (qk_scr[5:5 + L, :] * cw[0:1, :] + qk_scr[6:6 + L, :] * cw[1:2, :]
            + qk_scr[7:7 + L, :] * cw[2:3, :] + qk_pre * cw[3:4, :])
    qk_scr[0:8, :] = qk_scr[L:L + 8, :]
    qk = _silu(conv)
    q = qk[:, :A_QK]
    k = qk[:, A_QK:] * (A_DQK ** -0.5)

    outs = []
    for hd in range(A_HEADS):
        qh = q[:, hd * A_DQK:(hd + 1) * A_DQK]
        kh = k[:, hd * A_DQK:(hd + 1) * A_DQK]
        vh = v[:, hd * A_DV:(hd + 1) * A_DV]
        qb = qh.astype(BF16)
        kb = kh.astype(BF16)
        br = bb_row[hd:hd + 1, :]
        m_h = m_old[hd:hd + 1, 0:1]
        t = jnp.where(causal, br, -jnp.inf)
        mc = jnp.maximum(jnp.max(t, axis=1, keepdims=True), m_h)
        m_last = jnp.maximum(jnp.max(br, axis=1, keepdims=True), m_h)
        b_last = b_col[L - 1:L, hd:hd + 1]

        sc = _dot_nt(qb, kb)
        w = jnp.exp(t - mc) * sc
        den = jnp.sum(w, axis=1, keepdims=True)
        num = _dot(w.astype(BF16), vh.astype(BF16))
        inter = jnp.exp(m_h - mc)
        ct = ct_scr[hd]
        nrow = n_scr[hd:hd + 1, :]
        num = num + inter * _dot(qb, ct.astype(BF16))
        den = den + inter * jnp.sum(qh * nrow, axis=1, keepdims=True)
        ms = b_col[:, hd:hd + 1] + mc
        outs.append(num / jnp.maximum(jnp.abs(den), jnp.exp(-ms)))

        wj = jnp.exp(bb_col[:, hd:hd + 1] - m_last)
        decay = jnp.exp(m_h - m_last)
        ct_scr[hd] = decay * ct + _dot(kh.T.astype(BF16), (wj * vh).astype(BF16))
        n_scr[hd:hd + 1, :] = decay * nrow + jnp.sum(wj * kh, axis=0, keepdims=True)
        m_scr[hd:hd + 1, :] = jnp.broadcast_to(b_last + m_last, (1, LANES))

    hng = hng_ref[...]
    og = _sigmoid(o_pre)
    ys = []
    for hd in range(A_HEADS):
        sl = slice(hd * A_DV, (hd + 1) * A_DV)
        ht = og[:, sl] * outs[hd]
        ys.append(_rms(ht, hng[:, sl]) * _silu(z[:, sl]))
    y = jnp.concatenate(ys, axis=1).astype(BF16)
    hn = h + _dot(y, wout_ref[...])

    gate = _sigmoid(_dot(_rms(hn, png_ref[...]).astype(BF16), wpg_ref[...]))
    emb = _dot(p_ref[...].astype(BF16), wpp_ref[...])
    out_ref[...] = hn + emb * gate


def _mlstm_layer(h, p, layer, norm_g, w_in, conv_w, b_gate, hnorm_g, w_out, ple_norm, w_pg, w_pp):
    B, S, D = h.shape
    L = MLSTM_CHUNK
    assert S % L == 0 and L % 8 == 0
    P = p.shape[-1]
    wb = w_in.astype(BF16)
    o0 = 2 * A_QK
    wqk, wv, wo, wz = wb[:, :o0], wb[:, o0:o0 + A_VW], wb[:, o0 + A_VW:o0 + 2 * A_VW], wb[:, o0 + 2 * A_VW:o0 + 3 * A_VW]
    wgi = wb[:, o0 + 3 * A_VW:o0 + 3 * A_VW + A_HEADS]
    wgf = wb[:, o0 + 3 * A_VW + A_HEADS:]
    pad = jnp.zeros((D, LANES - A_HEADS), BF16)
    wg = jnp.concatenate([wgi, pad, wgf, pad], axis=1)
    bpad = jnp.zeros((LANES - A_HEADS,), F32)
    bg = jnp.concatenate([b_gate[:A_HEADS], bpad, b_gate[A_HEADS:], bpad])[None, :]

    row = lambda b, i: (b, i, 0)
    in_specs = [
        pl.BlockSpec((None, L, D), row),
        pl.BlockSpec((None, None, L, P), lambda b, i: (layer, b, i, 0)),
        _const_spec((1, D)),
        _const_spec((D, 2 * A_QK)), _const_spec((D, A_VW)), _const_spec((D, A_VW)), _const_spec((D, A_VW)),
        _const_spec((D, 2 * LANES)),
        _const_spec((A_CONV, 2 * A_QK)), _const_spec((1, 2 * LANES)), _const_spec((1, A_VW)),
        _const_spec((A_VW, D)), _const_spec((1, D)), _const_spec((D, D)), _const_spec((P, D)),
    ]
    return pl.pallas_call(
        _mlstm_body,
        grid=(B, S // L),
        in_specs=in_specs,
        out_specs=pl.BlockSpec((None, L, D), row),
        out_shape=jax.ShapeDtypeStruct((B, S, D), F32),
        scratch_shapes=[
            pltpu.VMEM((L + 8, 2 * A_QK), F32),
            pltpu.VMEM((A_HEADS, A_DQK, A_DV), F32),
            pltpu.VMEM((8, A_DQK), F32),
            pltpu.VMEM((8, LANES), F32),
        ],
        compiler_params=pltpu.CompilerParams(
            dimension_semantics=("arbitrary", "arbitrary"), vmem_limit_bytes=VMEM_LIMIT),
        name="mlstm_layer",
    )(h, p, norm_g[None, :], wqk, wv, wo, wz, wg, conv_w, bg, hnorm_g[None, :],
      w_out.astype(BF16), ple_norm[None, :], w_pg.astype(BF16), w_pp.astype(BF16))


def _norm_rope(x, g, cos2, sin2):
    xn = _rms(x, g)
    return xn * cos2 + pltpu.roll(xn, B_HD // 2, axis=1) * sin2


def _qkv_body(*refs, with_kv):
    if with_kv:
        (h_ref, cos_ref, sin_ref, ng_ref, win_ref, qg_ref, nkv_ref, wkv_ref, kg_ref,
         q0_ref, q1_ref, q2_ref, z_ref, k0_ref, k1_ref, k2_ref, v0_ref, v1_ref, v2_ref, slab) = refs
    else:
        (h_ref, cos_ref, sin_ref, ng_ref, win_ref, qg_ref, q0_ref, q1_ref, q2_ref, z_ref, slab) = refs
    n = h_ref.shape[0]

    def deinterleave(val, slot, cs, dests):
        slab[slot] = val
        for ref, r in dests:
            for c in range(r):
                ref[c, :, cs] = slab[slot, pl.ds(c, n // r, stride=r), :].astype(BF16)

    h = h_ref[...]
    cos2 = cos_ref[...]
    sin2 = sin_ref[...]
    xn = _rms(h, ng_ref[...]).astype(BF16)
    proj = _dot(xn, win_ref[...])
    qg = qg_ref[...]
    for hd in range(B_KV_HEADS):
        cs = slice(hd * B_HD, (hd + 1) * B_HD)
        q0_ref[:, cs] = _norm_rope(proj[:, hd * B_HD:(hd + 1) * B_HD], qg, cos2, sin2).astype(BF16)
        for g, (q_ref, r) in enumerate(((q1_ref, 4), (q2_ref, MAX_DIL)), start=1):
            c0 = (g * B_KV_HEADS + hd) * B_HD
            deinterleave(_norm_rope(proj[:, c0:c0 + B_HD], qg, cos2, sin2), (g - 1) * B_KV_HEADS + hd, cs,
                         ((q_ref, r),))
    z_ref[...] = proj[:, 3 * B_VW:]
    if with_kv:
        xk = _rms(h, nkv_ref[...]).astype(BF16)
        kv = _dot(xk, wkv_ref[...])
        kg = kg_ref[...]
        for hd in range(B_KV_HEADS):
            cs = slice(hd * B_HD, (hd + 1) * B_HD)
            kh = _norm_rope(kv[:, hd * B_HD:(hd + 1) * B_HD], kg, cos2, sin2)
            vh = kv[:, B_VW + hd * B_HD:B_VW + (hd + 1) * B_HD]
            k0_ref[:, cs] = kh.astype(BF16)
            v0_ref[:, cs] = vh.astype(BF16)
            deinterleave(kh, 2 * B_KV_HEADS + hd, cs, ((k1_ref, 4), (k2_ref, MAX_DIL)))
            deinterleave(vh, 3 * B_KV_HEADS + hd, cs, ((v1_ref, 4), (v2_ref, MAX_DIL)))


def _qkv_proj(h, cos2, sin2, norm_g, w_in, qnorm_g, kv_params=None):
    B, S, D = h.shape
    n = 512
    assert S % n == 0 and n % (8 * MAX_DIL) == 0
    with_kv = kv_params is not None

    def lay_specs():
        shapes, specs = [], []
        for _, r in DILATED_GROUPS:
            shapes.append(jax.ShapeDtypeStruct((B, r, S // r, B_VW), BF16))
            if r == 1:
                specs.append(pl.BlockSpec((None, None, n, B_VW), lambda b, i: (b, 0, i, 0)))
            else:
                specs.append(pl.BlockSpec((None, r, n // r, B_VW), lambda b, i: (b, 0, i, 0)))
        return shapes, specs

    row = lambda b, i: (b, i, 0)
    out_shape, out_specs = lay_specs()
    out_shape.append(jax.ShapeDtypeStruct((B, S, B_VW), F32))
    out_specs.append(pl.BlockSpec((None, n, B_VW), row))
    in_specs = [
        pl.BlockSpec((None, n, D), row),
        pl.BlockSpec((n, B_HD), lambda b, i: (i, 0)),
        pl.BlockSpec((n, B_HD), lambda b, i: (i, 0)),
        _const_spec((1, D)), _const_spec((D, 4 * B_VW)), _const_spec((1, B_HD)),
    ]
    args = [h, cos2, sin2, norm_g[None, :], w_in.astype(BF16), qnorm_g[None, :]]
    if with_kv:
        norm_kv, w_kv, knorm = kv_params
        in_specs += [_const_spec((1, D)), _const_spec((D, 2 * B_VW)), _const_spec((1, B_HD))]
        args += [norm_kv[None, :], w_kv.astype(BF16), knorm[None, :]]
        for _ in range(2):
            shp, sp = lay_specs()
            out_shape += shp
            out_specs += sp
    n_slots = (4 if with_kv else 2) * B_KV_HEADS
    outs = pl.pallas_call(
        functools.partial(_qkv_body, with_kv=with_kv),
        grid=(B, S // n),
        in_specs=in_specs,
        out_specs=out_specs,
        out_shape=out_shape,
        scratch_shapes=[pltpu.VMEM((n_slots, n, LANES), F32)],
        compiler_params=pltpu.CompilerParams(
            dimension_semantics=("arbitrary", "arbitrary"), vmem_limit_bytes=VMEM_LIMIT),
        name="qkv_proj_kv" if with_kv else "qkv_proj",
    )(*args)
    if with_kv:
        return outs[0:3], outs[3], outs[4:7], outs[7:10]
    return outs[0:3], outs[3]


def _attn_body(q_ref, kh_ref, kc_ref, vh_ref, vc_ref, o_ref, lse_ref, *scr, r):
    qblk = q_ref.shape[1]
    u0 = pl.program_id(1) * qblk
    nk = ATT_SUB + N_BACK
    qq = lax.broadcasted_iota(jnp.int32, (ATT_SUB, nk), 0)
    kk = lax.broadcasted_iota(jnp.int32, (ATT_SUB, nk), 1)
    dist = N_BACK + qq - kk
    band = (dist >= 0) & (dist <= N_BACK)
    lane = lax.broadcasted_iota(jnp.int32, (ATT_SUB, LANES), 1)
    scale = B_HD ** -0.5

    def residue(c):
        q = q_ref[c]
        k = jnp.concatenate([kh_ref[c], kc_ref[c]], axis=0)
        v = jnp.concatenate([vh_ref[c], vc_ref[c]], axis=0)
        for sb in range(qblk // ATT_SUB):
            r0 = sb * ATT_SUB
            kpos = u0 + (r0 - N_BACK) + kk
            bias = jnp.where(band & (kpos >= 0), 0.0, -jnp.inf)
            lse_blk = jnp.zeros((ATT_SUB, LANES), F32)
            for hd in range(B_KV_HEADS):
                cs = slice(hd * B_HD, (hd + 1) * B_HD)
                s = _dot_nt(q[r0:r0 + ATT_SUB, cs], k[r0:r0 + nk, cs]) * scale + bias
                mx = jnp.max(s, axis=1, keepdims=True)
                e = jnp.exp(s - mx)
                den = jnp.sum(e, axis=1, keepdims=True)
                o = _dot(e.astype(BF16), v[r0:r0 + nk, cs]) / den
                lse_blk = jnp.where(lane == hd, mx + jnp.log(den), lse_blk)
                if r == 1:
                    o_ref[r0:r0 + ATT_SUB, cs] = o.astype(BF16)
                else:
                    scr[0][hd, pl.ds(r0 * r + c, ATT_SUB, stride=r), :] = o
            if r == 1:
                lse_ref[r0:r0 + ATT_SUB, :] = lse_blk
            else:
                scr[1][pl.ds(r0 * r + c, ATT_SUB, stride=r), :] = lse_blk

    if r <= 4:
        for c in range(r):
            residue(c)
    else:
        def loop_body(c, carry):
            residue(c)
            return carry
        lax.fori_loop(0, r, loop_body, 0)
    if r > 1:
        for hd in range(B_KV_HEADS):
            o_ref[:, hd * B_HD:(hd + 1) * B_HD] = scr[0][hd].astype(BF16)
        lse_ref[...] = scr[1][...]


def _window_attn(q, k, v):
    B, r, N, W = q.shape
    qblk = {1: 512, 4: 256}.get(r, ATT_SUB)
    assert N % qblk == 0
    per = qblk // N_BACK
    tt = qblk * r
    cur = lambda b, i: (b, 0, i, 0)
    halo = lambda b, i: (b, 0, jnp.maximum(i * per - 1, 0), 0)
    row = lambda b, i: (b, i, 0)
    scratch = [] if r == 1 else [pltpu.VMEM((B_KV_HEADS, tt, LANES), F32), pltpu.VMEM((tt, LANES), F32)]
    return pl.pallas_call(
        functools.partial(_attn_body, r=r),
        grid=(B, N // qblk),
        in_specs=[
            pl.BlockSpec((None, r, qblk, W), cur),
            pl.BlockSpec((None, r, N_BACK, W), halo),
            pl.BlockSpec((None, r, qblk, W), cur),
            pl.BlockSpec((None, r, N_BACK, W), halo),
            pl.BlockSpec((None, r, qblk, W), cur),
        ],
        out_specs=[pl.BlockSpec((None, tt, W), row), pl.BlockSpec((None, tt, LANES), row)],
        out_shape=[
            jax.ShapeDtypeStruct((B, N * r, W), BF16),
            jax.ShapeDtypeStruct((B, N * r, LANES), F32),
        ],
        scratch_shapes=scratch,
        compiler_params=pltpu.CompilerParams(
            dimension_semantics=("arbitrary", "arbitrary"), vmem_limit_bytes=VMEM_LIMIT),
        name=f"window_attn_r{r}",
    )(q, k, k, v, v)


def _post_body(h_ref, p_ref, z_ref, o0_ref, o1_ref, o2_ref, l0_ref, l1_ref, l2_ref,
               wout_ref, png_ref, wpg_ref, wpp_ref, out_ref):
    l0, l1, l2 = l0_ref[...], l1_ref[...], l2_ref[...]
    mx = jnp.maximum(jnp.maximum(l0, l1), l2)
    e0, e1, e2 = jnp.exp(l0 - mx), jnp.exp(l1 - mx), jnp.exp(l2 - mx)
    tot = e0 + e1 + e2
    w0, w1, w2 = e0 / tot, e1 / tot, e2 / tot
    z = z_ref[...]
    ys = []
    for hd in range(B_KV_HEADS):
        cs = slice(hd * B_HD, (hd + 1) * B_HD)
        o = (w0[:, hd:hd + 1] * o0_ref[:, cs].astype(F32) + w1[:, hd:hd + 1] * o1_ref[:, cs].astype(F32)
             + w2[:, hd:hd + 1] * o2_ref[:, cs].astype(F32))
        ys.append(o * _silu(z[:, cs]))
    y = jnp.concatenate(ys, axis=1).astype(BF16)
    hn = h_ref[...] + _dot(y, wout_ref[...])
    gate = _sigmoid(_dot(_rms(hn, png_ref[...]).astype(BF16), wpg_ref[...]))
    emb = _dot(p_ref[...].astype(BF16), wpp_ref[...])
    out_ref[...] = hn + emb * gate


def _attn_post(h, p, layer, z, os_, lses, w_out, ple_norm, w_pg, w_pp):
    B, S, D = h.shape
    P = p.shape[-1]
    n = 512
    assert S % n == 0
    row = lambda b, i: (b, i, 0)
    in_specs = [
        pl.BlockSpec((None, n, D), row),
        pl.BlockSpec((None, None, n, P), lambda b, i: (layer, b, i, 0)),
        pl.BlockSpec((None, n, B_VW), row),
        pl.BlockSpec((None, n, B_VW), row), pl.BlockSpec((None, n, B_VW), row), pl.BlockSpec((None, n, B_VW), row),
        pl.BlockSpec((None, n, LANES), row), pl.BlockSpec((None, n, LANES), row), pl.BlockSpec((None, n, LANES), row),
        _const_spec((B_VW, D)), _const_spec((1, D)), _const_spec((D, D)), _const_spec((P, D)),
    ]
    return pl.pallas_call(
        _post_body,
        grid=(B, S // n),
        in_specs=in_specs,
        out_specs=pl.BlockSpec((None, n, D), row),
        out_shape=jax.ShapeDtypeStruct((B, S, D), F32),
        compiler_params=pltpu.CompilerParams(
            dimension_semantics=("arbitrary", "arbitrary"), vmem_limit_bytes=VMEM_LIMIT),
        name="attn_post",
    )(h, p, z, *os_, *lses, w_out.astype(BF16), ple_norm[None, :], w_pg.astype(BF16), w_pp.astype(BF16))


def _rope_tables(seq):
    inv = ROPE_THETA ** (-jnp.arange(0, B_HD, 2, dtype=F32) / B_HD)
    ang = jnp.arange(seq, dtype=F32)[:, None] * inv[None, :]
    cos, sin = jnp.cos(ang), jnp.sin(ang)
    return jnp.concatenate([cos, cos], axis=1), jnp.concatenate([-sin, sin], axis=1)


def kernel(x, p, norm_a, w_in_a, conv_a, b_gate_a, hnorm_a, w_out_a, norm_kv, w_kv, knorm, norm_b, w_in_b, qnorm_b, w_out_b, ple_norm, w_ple_gate, w_ple):
    S = x.shape[1]
    n_a = norm_a.shape[0]
    n_b = norm_b.shape[0]
    assert all(w // d == N_BACK for w, d in DILATED_GROUPS)
    cos2, sin2 = _rope_tables(S)
    h = x
    for l in range(n_a):
        h = _mlstm_layer(h, p, l, norm_a[l], w_in_a[l], conv_a[l], b_gate_a[l], hnorm_a[l], w_out_a[l],
                         ple_norm[l], w_ple_gate[l], w_ple[l])
    ks = vs = None
    for j in range(n_b):
        l = n_a + j
        if j == 0:
            qs, z, ks, vs = _qkv_proj(h, cos2, sin2, norm_b[j], w_in_b[j], qnorm_b[j], (norm_kv, w_kv, knorm))
        else:
            qs, z = _qkv_proj(h, cos2, sin2, norm_b[j], w_in_b[j], qnorm_b[j])
        res = [_window_attn(qs[g], ks[g], vs[g]) for g in range(len(DILATED_GROUPS))]
        h = _attn_post(h, p, l, z, [r[0] for r in res], [r[1] for r in res],
                       w_out_b[j], ple_norm[l], w_ple_gate[l], w_ple[l])
    return h
```

```python
import functools

import jax
import jax.numpy as jnp
from jax import lax
from jax.experimental import pallas as pl
from jax.experimental.pallas import tpu as pltpu

F32 = jnp.float32
BF16 = jnp.bfloat16

EPS = 1e-6
ROPE_THETA = 10000.0

A_HEADS = 4
A_DQK = 128
A_DV = 256
A_QK = A_HEADS * A_DQK
A_VW = A_HEADS * A_DV
A_CONV = 4
MLSTM_CHUNK = 128
MLSTM_TILE = 512

B_KV_HEADS = 4
B_HD = 128
B_VW = B_KV_HEADS * B_HD
DILATED_GROUPS = ((128, 1), (512, 4), (2048, 16))
N_BACK = 128
MAX_DIL = 16
ATT_SUB = 128

LANES = 128
VMEM_LIMIT = 56 * 1024 * 1024


def _dot(a, b):
    return jnp.dot(a, b, preferred_element_type=F32)


def _dot_nt(a, b):
    return lax.dot_general(a, b, (((1,), (1,)), ((), ())), preferred_element_type=F32)


def _rms(x, g):
    ms = jnp.mean(x * x, axis=-1, keepdims=True)
    return x * lax.rsqrt(ms + EPS) * g


def _sigmoid(x):
    return 1.0 / (1.0 + jnp.exp(-x))


def _silu(x):
    return x * _sigmoid(x)


def _log_sigmoid(x):
    return jnp.minimum(x, 0.0) - jnp.log1p(jnp.exp(-jnp.abs(x)))


def _const_spec(shape):
    nd = len(shape)
    return pl.BlockSpec(shape, lambda *_: (0,) * nd, pipeline_mode=pl.Buffered(1))


def _mlstm_body(h_ref, p_ref, ng_ref, wqk_ref, wv_ref, wo_ref, wz_ref, wg_ref, conv_ref, bg_ref,
                hng_ref, wout_ref, png_ref, wpg_ref, wpp_ref, out_ref,
                qk_scr, hh_scr, ct_scr, n_scr, m_scr):
    T = h_ref.shape[0]
    L = MLSTM_CHUNK
    step = pl.program_id(1)

    @pl.when(step == 0)
    def _():
        qk_scr[0:8, :] = jnp.zeros((8, 2 * A_QK), F32)
        ct_scr[...] = jnp.zeros(ct_scr.shape, F32)
        n_scr[...] = jnp.zeros(n_scr.shape, F32)
        m_scr[...] = jnp.zeros(m_scr.shape, F32)

    h = h_ref[...]
    xn = _rms(h, ng_ref[...]).astype(BF16)
    g2 = _dot(xn, wg_ref[...]) + bg_ref[...]
    gi = g2[:, :LANES]
    gf = g2[:, LANES:]

    row_i = lax.broadcasted_iota(jnp.int32, (L, L), 0)
    col_i = lax.broadcasted_iota(jnp.int32, (L, L), 1)
    causal = col_i <= row_i
    tril = jnp.where(causal, 1.0, 0.0).astype(BF16)
    lf = _log_sigmoid(gf)
    lf1 = lf.astype(BF16)
    r1 = lf - lf1.astype(F32)
    lf2 = r1.astype(BF16)
    lf3 = (r1 - lf2.astype(F32)).astype(BF16)
    b_col = jnp.concatenate(
        [_dot(tril, lf1[c * L:(c + 1) * L]) + _dot(tril, lf2[c * L:(c + 1) * L]) + _dot(tril, lf3[c * L:(c + 1) * L])
         for c in range(T // L)], axis=0)
    bb_col = gi - b_col
    bb_row = bb_col.T[0:8, :]
    m_old = m_scr[...]

    qk_pre = _dot(xn, wqk_ref[...])
    v = _dot(xn, wv_ref[...])
    o_pre = _dot(xn, wo_ref[...])
    z = _dot(xn, wz_ref[...])

    qk_scr[8:8 + T, :] = qk_pre
    cw = conv_ref[...]
    conv = (qk_scr[5:5 + T, :] * cw[0:1, :] + qk_scr[6:6 + T, :] * cw[1:2, :]
            + qk_scr[7:7 + T, :] * cw[2:3, :] + qk_pre * cw[3:4, :])
    qk_scr[0:8, :] = qk_scr[T:T + 8, :]
    qk = _silu(conv)
    q = qk[:, :A_QK]
    k = qk[:, A_QK:] * (A_DQK ** -0.5)

    for hd in range(A_HEADS):
        m_h = m_old[hd:hd + 1, 0:1]
        ct = ct_scr[hd]
        nrow = n_scr[hd:hd + 1, :]
        for c in range(T // L):
            rs = slice(c * L, (c + 1) * L)
            qh = q[rs, hd * A_DQK:(hd + 1) * A_DQK]
            kh = k[rs, hd * A_DQK:(hd + 1) * A_DQK]
            vh = v[rs, hd * A_DV:(hd + 1) * A_DV]
            qb = qh.astype(BF16)
            br = bb_row[hd:hd + 1, rs]
            t = jnp.where(causal, br, -jnp.inf)
            mc = jnp.maximum(jnp.max(t, axis=1, keepdims=True), m_h)
            m_last = jnp.maximum(jnp.max(br, axis=1, keepdims=True), m_h)
            b_last = b_col[(c + 1) * L - 1:(c + 1) * L, hd:hd + 1]

            w = jnp.exp(t - mc) * _dot_nt(qb, kh.astype(BF16))
            inter = jnp.exp(m_h - mc)
            num = _dot(w.astype(BF16), vh.astype(BF16)) + inter * _dot(qb, ct.astype(BF16))
            den = jnp.sum(w, axis=1, keepdims=True) + inter * jnp.sum(qh * nrow, axis=1, keepdims=True)
            floor = jnp.exp(-(b_col[rs, hd:hd + 1] + mc))
            hh_scr[rs, hd * A_DV:(hd + 1) * A_DV] = num * (1.0 / jnp.maximum(jnp.abs(den), floor))

            wj = jnp.exp(bb_col[rs, hd:hd + 1] - m_last)
            decay = jnp.exp(m_h - m_last)
            ct = decay * ct + _dot(kh.T.astype(BF16), (wj * vh).astype(BF16))
            nrow = decay * nrow + jnp.sum(wj * kh, axis=0, keepdims=True)
            m_h = b_last + m_last
        ct_scr[hd] = ct
        n_scr[hd:hd + 1, :] = nrow
        m_scr[hd:hd + 1, :] = jnp.broadcast_to(m_h, (1, LANES))

    hng = hng_ref[...]
    half = T // 2
    for r0 in (0, half):
        rs = slice(r0, r0 + half)
        og = _sigmoid(o_pre[rs])
        ys = []
        for hd in range(A_HEADS):
            sl = slice(hd * A_DV, (hd + 1) * A_DV)
            ht = og[:, sl] * hh_scr[rs, sl]
            ys.append(_rms(ht, hng[:, sl]) * _silu(z[rs, sl]))
        y = jnp.concatenate(ys, axis=1).astype(BF16)
        hn = h[rs] + _dot(y, wout_ref[...])
        gate = _sigmoid(_dot(_rms(hn, png_ref[...]).astype(BF16), wpg_ref[...]))
        emb = _dot(p_ref[rs, :].astype(BF16), wpp_ref[...])
        out_ref[rs, :] = hn + emb * gate


def _mlstm_layer(h, p, layer, norm_g, w_in, conv_w, b_gate, hnorm_g, w_out, ple_norm, w_pg, w_pp):
    B, S, D = h.shape
    L = MLSTM_TILE
    assert S % L == 0 and L % (2 * MLSTM_CHUNK) == 0
    P = p.shape[-1]
    wb = w_in.astype(BF16)
    o0 = 2 * A_QK
    wqk, wv, wo, wz = wb[:, :o0], wb[:, o0:o0 + A_VW], wb[:, o0 + A_VW:o0 + 2 * A_VW], wb[:, o0 + 2 * A_VW:o0 + 3 * A_VW]
    wgi = wb[:, o0 + 3 * A_VW:o0 + 3 * A_VW + A_HEADS]
    wgf = wb[:, o0 + 3 * A_VW + A_HEADS:]
    pad = jnp.zeros((D, LANES - A_HEADS), BF16)
    wg = jnp.concatenate([wgi, pad, wgf, pad], axis=1)
    bpad = jnp.zeros((LANES - A_HEADS,), F32)
    bg = jnp.concatenate([b_gate[:A_HEADS], bpad, b_gate[A_HEADS:], bpad])[None, :]

    row = lambda b, i: (b, i, 0)
    in_specs = [
        pl.BlockSpec((None, L, D), row),
        pl.BlockSpec((None, None, L, P), lambda b, i: (layer, b, i, 0)),
        _const_spec((1, D)),
        _const_spec((D, 2 * A_QK)), _const_spec((D, A_VW)), _const_spec((D, A_VW)), _const_spec((D, A_VW)),
        _const_spec((D, 2 * LANES)),
        _const_spec((A_CONV, 2 * A_QK)), _const_spec((1, 2 * LANES)), _const_spec((1, A_VW)),
        _const_spec((A_VW, D)), _const_spec((1, D)), _const_spec((D, D)), _const_spec((P, D)),
    ]
    return pl.pallas_call(
        _mlstm_body,
        grid=(B, S // L),
        in_specs=in_specs,
        out_specs=pl.BlockSpec((None, L, D), row),
        out_shape=jax.ShapeDtypeStruct((B, S, D), F32),
        scratch_shapes=[
            pltpu.VMEM((L + 8, 2 * A_QK), F32),
            pltpu.VMEM((L, A_VW), F32),
            pltpu.VMEM((A_HEADS, A_DQK, A_DV), F32),
            pltpu.VMEM((8, A_DQK), F32),
            pltpu.VMEM((8, LANES), F32),
        ],
        compiler_params=pltpu.CompilerParams(
            dimension_semantics=("arbitrary", "arbitrary"), vmem_limit_bytes=VMEM_LIMIT),
        name="mlstm_layer",
    )(h, p, norm_g[None, :], wqk, wv, wo, wz, wg, conv_w, bg, hnorm_g[None, :],
      w_out.astype(BF16), ple_norm[None, :], w_pg.astype(BF16), w_pp.astype(BF16))


def _head_rms(x, g4, ones_bd):
    ss = _dot((x * x).astype(BF16), ones_bd)
    return x * lax.rsqrt(ss * (1.0 / B_HD) + EPS) * g4


def _rope(xn, cos2, sin2):
    return xn * cos2 + pltpu.roll(xn, B_HD // 2, axis=1) * sin2


def _qkv_body(*refs, with_kv):
    if with_kv:
        (h_ref, cos_ref, sin_ref, ones_ref, ng_ref, win_ref, qg_ref, nkv_ref, wkv_ref, kg_ref,
         q0_ref, q1_ref, q2_ref, z_ref, k0_ref, k1_ref, k2_ref, v0_ref, v1_ref, v2_ref, *slab) = refs
    else:
        (h_ref, cos_ref, sin_ref, ones_ref, ng_ref, win_ref, qg_ref, q0_ref, q1_ref, q2_ref, z_ref, *slab) = refs
    n = h_ref.shape[0]

    def deinterleave(val, slot, cs, dests):
        slab[slot][...] = val
        for ref, r in dests:
            for c in range(r):
                ref[c, :, cs] = slab[slot][pl.ds(c, n // r, stride=r), :].astype(BF16)

    h = h_ref[...]
    cos2 = cos_ref[...]
    sin2 = sin_ref[...]
    xn = _rms(h, ng_ref[...]).astype(BF16)
    qg = qg_ref[...]
    ones_bd = ones_ref[...]
    heads = [slice(hd * B_HD, (hd + 1) * B_HD) for hd in range(B_KV_HEADS)]

    pq = _head_rms(_dot(xn, win_ref[:, 2 * B_VW:3 * B_VW]), qg, ones_bd)
    for hd, cs in enumerate(heads):
        deinterleave(_rope(pq[:, cs], cos2, sin2), B_KV_HEADS + hd, cs, ((q2_ref, MAX_DIL),))
    if with_kv:
        xk = _rms(h, nkv_ref[...]).astype(BF16)
        pk = _head_rms(_dot(xk, wkv_ref[:, :B_VW]), kg_ref[...], ones_bd)
        for hd, cs in enumerate(heads):
            kh = _rope(pk[:, cs], cos2, sin2)
            k0_ref[:, cs] = kh.astype(BF16)
            deinterleave(kh, 2 * B_KV_HEADS + hd, cs, ((k1_ref, 4), (k2_ref, MAX_DIL)))
        pv = _dot(xk, wkv_ref[:, B_VW:])
        for hd, cs in enumerate(heads):
            v0_ref[:, cs] = pv[:, cs].astype(BF16)
            deinterleave(pv[:, cs], 3 * B_KV_HEADS + hd, cs, ((v1_ref, 4), (v2_ref, MAX_DIL)))
    pq = _head_rms(_dot(xn, win_ref[:, B_VW:2 * B_VW]), qg, ones_bd)
    for hd, cs in enumerate(heads):
        deinterleave(_rope(pq[:, cs], cos2, sin2), hd, cs, ((q1_ref, 4),))
    pq = _head_rms(_dot(xn, win_ref[:, :B_VW]), qg, ones_bd)
    for hd, cs in enumerate(heads):
        q0_ref[:, cs] = _rope(pq[:, cs], cos2, sin2).astype(BF16)
    z_ref[...] = _dot(xn, win_ref[:, 3 * B_VW:])


def _qkv_proj(h, cos2, sin2, norm_g, w_in, qnorm_g, kv_params=None):
    B, S, D = h.shape
    n = 512
    assert S % n == 0 and n % (8 * MAX_DIL) == 0
    with_kv = kv_params is not None

    def lay_specs():
        shapes, specs = [], []
        for _, r in DILATED_GROUPS:
            shapes.append(jax.ShapeDtypeStruct((B, r, S // r, B_VW), BF16))
            if r == 1:
                specs.append(pl.BlockSpec((None, None, n, B_VW), lambda b, i: (b, 0, i, 0)))
            else:
                specs.append(pl.BlockSpec((None, r, n // r, B_VW), lambda b, i: (b, 0, i, 0)))
        return shapes, specs

    row = lambda b, i: (b, i, 0)
    out_shape, out_specs = lay_specs()
    out_shape.append(jax.ShapeDtypeStruct((B, S, B_VW), F32))
    out_specs.append(pl.BlockSpec((None, n, B_VW), row))
    in_specs = [
        pl.BlockSpec((None, n, D), row),
        pl.BlockSpec((n, B_HD), lambda b, i: (i, 0)),
        pl.BlockSpec((n, B_HD), lambda b, i: (i, 0)),
        _const_spec((B_VW, B_VW)),
        _const_spec((1, D)), _const_spec((D, 4 * B_VW)), _const_spec((1, B_VW)),
    ]
    head_id = jnp.arange(B_VW) // B_HD
    ones_bd = (head_id[:, None] == head_id[None, :]).astype(BF16)
    tile4 = lambda g: jnp.tile(g, B_KV_HEADS)[None, :]
    args = [h, cos2, sin2, ones_bd, norm_g[None, :], w_in.astype(BF16), tile4(qnorm_g)]
    if with_kv:
        norm_kv, w_kv, knorm = kv_params
        in_specs += [_const_spec((1, D)), _const_spec((D, 2 * B_VW)), _const_spec((1, B_VW))]
        args += [norm_kv[None, :], w_kv.astype(BF16), tile4(knorm)]
        for _ in range(2):
            shp, sp = lay_specs()
            out_shape += shp
            out_specs += sp
    n_slots = (4 if with_kv else 2) * B_KV_HEADS
    outs = pl.pallas_call(
        functools.partial(_qkv_body, with_kv=with_kv),
        grid=(B, S // n),
        in_specs=in_specs,
        out_specs=out_specs,
        out_shape=out_shape,
        scratch_shapes=[pltpu.VMEM((n, LANES), F32) for _ in range(n_slots)],
        compiler_params=pltpu.CompilerParams(
            dimension_semantics=("arbitrary", "arbitrary"), vmem_limit_bytes=VMEM_LIMIT),
        name="qkv_proj_kv" if with_kv else "qkv_proj",
    )(*args)
    if with_kv:
        return outs[0:3], outs[3], outs[4:7], outs[7:10]
    return outs[0:3], outs[3]


def _attn_body(q_ref, kh_ref, kc_ref, vh_ref, vc_ref, o_ref, lse_ref, *scr, r):
    qblk = q_ref.shape[1]
    u0 = pl.program_id(1) * qblk
    nk = ATT_SUB + N_BACK
    qq = lax.broadcasted_iota(jnp.int32, (ATT_SUB, nk), 0)
    kk = lax.broadcasted_iota(jnp.int32, (ATT_SUB, nk), 1)
    dist = N_BACK + qq - kk
    band = (dist >= 0) & (dist <= N_BACK)
    lane = lax.broadcasted_iota(jnp.int32, (ATT_SUB, LANES), 1)
    scale = B_HD ** -0.5

    def residue(c):
        q = q_ref[c]
        k = jnp.concatenate([kh_ref[c], kc_ref[c]], axis=0)
        v = jnp.concatenate([vh_ref[c], vc_ref[c]], axis=0)
        for sb in range(qblk // ATT_SUB):
            r0 = sb * ATT_SUB
            kpos = u0 + (r0 - N_BACK) + kk
            bias = jnp.where(band & (kpos >= 0), 0.0, -jnp.inf)
            lse_blk = jnp.zeros((ATT_SUB, LANES), F32)
            for hd in range(B_KV_HEADS):
                cs = slice(hd * B_HD, (hd + 1) * B_HD)
                s = _dot_nt(q[r0:r0 + ATT_SUB, cs], k[r0:r0 + nk, cs]) * scale + bias
                mx = jnp.max(s, axis=1, keepdims=True)
                e = jnp.exp(s - mx)
                den = jnp.sum(e, axis=1, keepdims=True)
                o = _dot(e.astype(BF16), v[r0:r0 + nk, cs]) * (1.0 / den)
                lse_blk = jnp.where(lane == hd, mx + jnp.log(den), lse_blk)
                if r == 1:
                    o_ref[r0:r0 + ATT_SUB, cs] = o.astype(BF16)
                else:
                    scr[0][hd, pl.ds(r0 * r + c, ATT_SUB, stride=r), :] = o
            if r == 1:
                lse_ref[r0:r0 + ATT_SUB, :] = lse_blk
            else:
                scr[1][pl.ds(r0 * r + c, ATT_SUB, stride=r), :] = lse_blk

    if r <= 4:
        for c in range(r):
            residue(c)
    else:
        def loop_body(c, carry):
            residue(c)
            return carry
        lax.fori_loop(0, r, loop_body, 0, unroll=4)
    if r > 1:
        for hd in range(B_KV_HEADS):
            o_ref[:, hd * B_HD:(hd + 1) * B_HD] = scr[0][hd].astype(BF16)
        lse_ref[...] = scr[1][...]


def _window_attn(q, k, v):
    B, r, N, W = q.shape
    qblk = {1: 512, 4: 256}.get(r, ATT_SUB)
    assert N % qblk == 0
    per = qblk // N_BACK
    tt = qblk * r
    cur = lambda b, i: (b, 0, i, 0)
    halo = lambda b, i: (b, 0, jnp.maximum(i * per - 1, 0), 0)
    row = lambda b, i: (b, i, 0)
    scratch = [] if r == 1 else [pltpu.VMEM((B_KV_HEADS, tt, LANES), F32), pltpu.VMEM((tt, LANES), F32)]
    return pl.pallas_call(
        functools.partial(_attn_body, r=r),
        grid=(B, N // qblk),
        in_specs=[
            pl.BlockSpec((None, r, qblk, W), cur),
            pl.BlockSpec((None, r, N_BACK, W), halo),
            pl.BlockSpec((None, r, qblk, W), cur),
            pl.BlockSpec((None, r, N_BACK, W), halo),
            pl.BlockSpec((None, r, qblk, W), cur),
        ],
        out_specs=[pl.BlockSpec((None, tt, W), row), pl.BlockSpec((None, tt, LANES), row)],
        out_shape=[
            jax.ShapeDtypeStruct((B, N * r, W), BF16),
            jax.ShapeDtypeStruct((B, N * r, LANES), F32),
        ],
        scratch_shapes=scratch,
        compiler_params=pltpu.CompilerParams(
            dimension_semantics=("arbitrary", "arbitrary"), vmem_limit_bytes=VMEM_LIMIT),
        name=f"window_attn_r{r}",
    )(q, k, k, v, v)


def _post_body(h_ref, p_ref, z_ref, o0_ref, o1_ref, o2_ref, l0_ref, l1_ref, l2_ref,
               wout_ref, png_ref, wpg_ref, wpp_ref, out_ref):
    half = h_ref.shape[0] // 2
    for r0 in (0, half):
        rs = slice(r0, r0 + half)
        l0, l1, l2 = l0_ref[rs, :], l1_ref[rs, :], l2_ref[rs, :]
        mx = jnp.maximum(jnp.maximum(l0, l1), l2)
        e0, e1, e2 = jnp.exp(l0 - mx), jnp.exp(l1 - mx), jnp.exp(l2 - mx)
        tot = e0 + e1 + e2
        w0, w1, w2 = e0 / tot, e1 / tot, e2 / tot
        ys = []
        for hd in range(B_KV_HEADS):
            cs = slice(hd * B_HD, (hd + 1) * B_HD)
            o = (w0[:, hd:hd + 1] * o0_ref[rs, cs].astype(F32) + w1[:, hd:hd + 1] * o1_ref[rs, cs].astype(F32)
                 + w2[:, hd:hd + 1] * o2_ref[rs, cs].astype(F32))
            ys.append(o * _silu(z_ref[rs, cs]))
        y = jnp.concatenate(ys, axis=1).astype(BF16)
        hn = h_ref[rs, :] + _dot(y, wout_ref[...])
        gate = _sigmoid(_dot(_rms(hn, png_ref[...]).astype(BF16), wpg_ref[...]))
        emb = _dot(p_ref[rs, :].astype(BF16), wpp_ref[...])
        out_ref[rs, :] = hn + emb * gate


def _attn_post(h, p, layer, z, os_, lses, w_out, ple_norm, w_pg, w_pp):
    B, S, D = h.shape
    P = p.shape[-1]
    n = 512
    assert S % n == 0
    row = lambda b, i: (b, i, 0)
    in_specs = [
        pl.BlockSpec((None, n, D), row),
        pl.BlockSpec((None, None, n, P), lambda b, i: (layer, b, i, 0)),
        pl.BlockSpec((None, n, B_VW), row),
        pl.BlockSpec((None, n, B_VW), row), pl.BlockSpec((None, n, B_VW), row), pl.BlockSpec((None, n, B_VW), row),
        pl.BlockSpec((None, n, LANES), row), pl.BlockSpec((None, n, LANES), row), pl.BlockSpec((None, n, LANES), row),
        _const_spec((B_VW, D)), _const_spec((1, D)), _const_spec((D, D)), _const_spec((P, D)),
    ]
    return pl.pallas_call(
        _post_body,
        grid=(B, S // n),
        in_specs=in_specs,
        out_specs=pl.BlockSpec((None, n, D), row),
        out_shape=jax.ShapeDtypeStruct((B, S, D), F32),
        compiler_params=pltpu.CompilerParams(
            dimension_semantics=("arbitrary", "arbitrary"), vmem_limit_bytes=VMEM_LIMIT),
        name="attn_post",
    )(h, p, z, *os_, *lses, w_out.astype(BF16), ple_norm[None, :], w_pg.astype(BF16), w_pp.astype(BF16))


def _rope_tables(seq):
    inv = ROPE_THETA ** (-jnp.arange(0, B_HD, 2, dtype=F32) / B_HD)
    ang = jnp.arange(seq, dtype=F32)[:, None] * inv[None, :]
    cos, sin = jnp.cos(ang), jnp.sin(ang)
    return jnp.concatenate([cos, cos], axis=1), jnp.concatenate([-sin, sin], axis=1)


def kernel(x, p, norm_a, w_in_a, conv_a, b_gate_a, hnorm_a, w_out_a, norm_kv, w_kv, knorm, norm_b, w_in_b, qnorm_b, w_out_b, ple_norm, w_ple_gate, w_ple):
    S = x.shape[1]
    n_a = norm_a.shape[0]
    n_b = norm_b.shape[0]
    assert all(w // d == N_BACK for w, d in DILATED_GROUPS)
    cos2, sin2 = _rope_tables(S)
    h = x
    for l in range(n_a):
        h = _mlstm_layer(h, p, l, norm_a[l], w_in_a[l], conv_a[l], b_gate_a[l], hnorm_a[l], w_out_a[l],
                         ple_norm[l], w_ple_gate[l], w_ple[l])
    ks = vs = None
    for j in range(n_b):
        l = n_a + j
        if j == 0:
            qs, z, ks, vs = _qkv_proj(h, cos2, sin2, norm_b[j], w_in_b[j], qnorm_b[j], (norm_kv, w_kv, knorm))
        else:
            qs, z = _qkv_proj(h, cos2, sin2, norm_b[j], w_in_b[j], qnorm_b[j])
        res = [_window_attn(qs[g], ks[g], vs[g]) for g in range(len(DILATED_GROUPS))]
        h = _attn_post(h, p, l, z, [r[0] for r in res], [r[1] for r in res],
                       w_out_b[j], ple_norm[l], w_ple_gate[l], w_ple[l])
    return h
```

```python
import functools

import jax
import jax.numpy as jnp
from jax import lax
from jax.experimental import pallas as pl
from jax.experimental.pallas import tpu as pltpu

F32 = jnp.float32
BF16 = jnp.bfloat16

EPS = 1e-6
ROPE_THETA = 10000.0

A_HEADS = 4
A_DQK = 128
A_DV = 256
A_QK = A_HEADS * A_DQK
A_VW = A_HEADS * A_DV
A_CONV = 4
MLSTM_CHUNK = 128
MLSTM_TILE = 512

B_KV_HEADS = 4
B_HD = 128
B_VW = B_KV_HEADS * B_HD
DILATED_GROUPS = ((128, 1), (512, 4), (2048, 16))
N_BACK = 128
MAX_DIL = 16
ATT_SUB = 128

LANES = 128
VMEM_LIMIT = 56 * 1024 * 1024


def _dot(a, b):
    return jnp.dot(a, b, preferred_element_type=F32)


def _dot_nt(a, b):
    return lax.dot_general(a, b, (((1,), (1,)), ((), ())), preferred_element_type=F32)


def _rms(x, g):
    ms = jnp.mean(x * x, axis=-1, keepdims=True)
    return x * lax.rsqrt(ms + EPS) * g


def _sigmoid(x):
    return 1.0 / (1.0 + jnp.exp(-x))


def _silu(x):
    return x * _sigmoid(x)


def _log_sigmoid(x):
    return jnp.minimum(x, 0.0) - jnp.log1p(jnp.exp(-jnp.abs(x)))


def _const_spec(shape):
    nd = len(shape)
    return pl.BlockSpec(shape, lambda *_: (0,) * nd, pipeline_mode=pl.Buffered(1))


def _mlstm_body(h_ref, p_ref, ng_ref, wqk_ref, wv_ref, wo_ref, wz_ref, wg_ref, conv_ref, bg_ref,
                hng_ref, wout_ref, png_ref, wpg_ref, wpp_ref, out_ref,
                qk_scr, hh_scr, ct_scr, n_scr, m_scr):
    T = h_ref.shape[0]
    L = MLSTM_CHUNK
    step = pl.program_id(1)

    @pl.when(step == 0)
    def _():
        qk_scr[0:8, :] = jnp.zeros((8, 2 * A_QK), F32)
        ct_scr[...] = jnp.zeros(ct_scr.shape, F32)
        n_scr[...] = jnp.zeros(n_scr.shape, F32)
        m_scr[...] = jnp.zeros(m_scr.shape, F32)

    h = h_ref[...]
    xn = _rms(h, ng_ref[...]).astype(BF16)
    g2 = _dot(xn, wg_ref[...]) + bg_ref[...]
    gi = g2[:, :LANES]
    gf = g2[:, LANES:]

    row_i = lax.broadcasted_iota(jnp.int32, (L, L), 0)
    col_i = lax.broadcasted_iota(jnp.int32, (L, L), 1)
    causal = col_i <= row_i
    tril = jnp.where(causal, 1.0, 0.0).astype(BF16)
    lf = _log_sigmoid(gf)
    lf1 = lf.astype(BF16)
    r1 = lf - lf1.astype(F32)
    lf2 = r1.astype(BF16)
    lf3 = (r1 - lf2.astype(F32)).astype(BF16)
    b_col = jnp.concatenate(
        [_dot(tril, lf1[c * L:(c + 1) * L]) + _dot(tril, lf2[c * L:(c + 1) * L]) + _dot(tril, lf3[c * L:(c + 1) * L])
         for c in range(T // L)], axis=0)
    bb_col = gi - b_col
    bb_row = bb_col.T[0:8, :]
    m_old = m_scr[...]

    qk_pre = _dot(xn, wqk_ref[...])
    v = _dot(xn, wv_ref[...])
    o_pre = _dot(xn, wo_ref[...])
    z = _dot(xn, wz_ref[...])

    qk_scr[8:8 + T, :] = qk_pre
    cw = conv_ref[...]
    conv = (qk_scr[5:5 + T, :] * cw[0:1, :] + qk_scr[6:6 + T, :] * cw[1:2, :]
            + qk_scr[7:7 + T, :] * cw[2:3, :] + qk_pre * cw[3:4, :])
    qk_scr[0:8, :] = qk_scr[T:T + 8, :]
    qk = _silu(conv)
    q = qk[:, :A_QK]
    k = qk[:, A_QK:] * (A_DQK ** -0.5)

    for hd in range(A_HEADS):
        m_h = m_old[hd:hd + 1, 0:1]
        ct = ct_scr[hd]
        nrow = n_scr[hd:hd + 1, :]
        for c in range(T // L):
            rs = slice(c * L, (c + 1) * L)
            qh = q[rs, hd * A_DQK:(hd + 1) * A_DQK]
            kh = k[rs, hd * A_DQK:(hd + 1) * A_DQK]
            vh = v[rs, hd * A_DV:(hd + 1) * A_DV]
            qb = qh.astype(BF16)
            br = bb_row[hd:hd + 1, rs]
            t = jnp.where(causal, br, -jnp.inf)
            mc = jnp.maximum(jnp.max(t, axis=1, keepdims=True), m_h)
            m_last = jnp.maximum(jnp.max(br, axis=1, keepdims=True), m_h)
            b_last = b_col[(c + 1) * L - 1:(c + 1) * L, hd:hd + 1]

            w = jnp.exp(t - mc) * _dot_nt(qb, kh.astype(BF16))
            inter = jnp.exp(m_h - mc)
            num = _dot(w.astype(BF16), vh.astype(BF16)) + inter * _dot(qb, ct.astype(BF16))
            den = jnp.sum(w, axis=1, keepdims=True) + inter * jnp.sum(qh * nrow, axis=1, keepdims=True)
            floor = jnp.exp(-(b_col[rs, hd:hd + 1] + mc))
            hh_scr[rs, hd * A_DV:(hd + 1) * A_DV] = num * (1.0 / jnp.maximum(jnp.abs(den), floor))

            wj = jnp.exp(bb_col[rs, hd:hd + 1] - m_last)
            decay = jnp.exp(m_h - m_last)
            ct = decay * ct + _dot(kh.T.astype(BF16), (wj * vh).astype(BF16))
            nrow = decay * nrow + jnp.sum(wj * kh, axis=0, keepdims=True)
            m_h = b_last + m_last
        ct_scr[hd] = ct
        n_scr[hd:hd + 1, :] = nrow
        m_scr[hd:hd + 1, :] = jnp.broadcast_to(m_h, (1, LANES))

    hng = hng_ref[...]
    half = T // 2
    for r0 in (0, half):
        rs = slice(r0, r0 + half)
        og = _sigmoid(o_pre[rs])
        ys = []
        for hd in range(A_HEADS):
            sl = slice(hd * A_DV, (hd + 1) * A_DV)
            ht = og[:, sl] * hh_scr[rs, sl]
            ys.append(_rms(ht, hng[:, sl]) * _silu(z[rs, sl]))
        y = jnp.concatenate(ys, axis=1).astype(BF16)
        hn = h[rs] + _dot(y, wout_ref[...])
        gate = _sigmoid(_dot(_rms(hn, png_ref[...]).astype(BF16), wpg_ref[...]))
        emb = _dot(p_ref[rs, :].astype(BF16), wpp_ref[...])
        out_ref[rs, :] = hn + emb * gate


def _mlstm_layer(h, p, layer, norm_g, w_in, conv_w, b_gate, hnorm_g, w_out, ple_norm, w_pg, w_pp):
    B, S, D = h.shape
    L = MLSTM_TILE
    assert S % L == 0 and L % (2 * MLSTM_CHUNK) == 0
    P = p.shape[-1]
    wb = w_in.astype(BF16)
    o0 = 2 * A_QK
    wqk, wv, wo, wz = wb[:, :o0], wb[:, o0:o0 + A_VW], wb[:, o0 + A_VW:o0 + 2 * A_VW], wb[:, o0 + 2 * A_VW:o0 + 3 * A_VW]
    wgi = wb[:, o0 + 3 * A_VW:o0 + 3 * A_VW + A_HEADS]
    wgf = wb[:, o0 + 3 * A_VW + A_HEADS:]
    pad = jnp.zeros((D, LANES - A_HEADS), BF16)
    wg = jnp.concatenate([wgi, pad, wgf, pad], axis=1)
    bpad = jnp.zeros((LANES - A_HEADS,), F32)
    bg = jnp.concatenate([b_gate[:A_HEADS], bpad, b_gate[A_HEADS:], bpad])[None, :]

    row = lambda b, i: (b, i, 0)
    in_specs = [
        pl.BlockSpec((None, L, D), row),
        pl.BlockSpec((None, None, L, P), lambda b, i: (layer, b, i, 0)),
        _const_spec((1, D)),
        _const_spec((D, 2 * A_QK)), _const_spec((D, A_VW)), _const_spec((D, A_VW)), _const_spec((D, A_VW)),
        _const_spec((D, 2 * LANES)),
        _const_spec((A_CONV, 2 * A_QK)), _const_spec((1, 2 * LANES)), _const_spec((1, A_VW)),
        _const_spec((A_VW, D)), _const_spec((1, D)), _const_spec((D, D)), _const_spec((P, D)),
    ]
    return pl.pallas_call(
        _mlstm_body,
        grid=(B, S // L),
        in_specs=in_specs,
        out_specs=pl.BlockSpec((None, L, D), row),
        out_shape=jax.ShapeDtypeStruct((B, S, D), F32),
        scratch_shapes=[
            pltpu.VMEM((L + 8, 2 * A_QK), F32),
            pltpu.VMEM((L, A_VW), F32),
            pltpu.VMEM((A_HEADS, A_DQK, A_DV), F32),
            pltpu.VMEM((8, A_DQK), F32),
            pltpu.VMEM((8, LANES), F32),
        ],
        compiler_params=pltpu.CompilerParams(
            dimension_semantics=("arbitrary", "arbitrary"), vmem_limit_bytes=VMEM_LIMIT),
        name="mlstm_layer",
    )(h, p, norm_g[None, :], wqk, wv, wo, wz, wg, conv_w, bg, hnorm_g[None, :],
      w_out.astype(BF16), ple_norm[None, :], w_pg.astype(BF16), w_pp.astype(BF16))


def _head_rms(x, g4, ones_bd):
    ss = _dot((x * x).astype(BF16), ones_bd)
    return x * lax.rsqrt(ss * (1.0 / B_HD) + EPS) * g4


def _rope(xn, cos2, sin2):
    return xn * cos2 + pltpu.roll(xn, B_HD // 2, axis=1) * sin2


def _qkv_body(*refs, with_kv):
    if with_kv:
        (h_ref, cos_ref, sin_ref, ones_ref, ng_ref, win_ref, qg_ref, nkv_ref, wkv_ref, kg_ref,
         q0_ref, q1_ref, q2_ref, z_ref, k0_ref, k1_ref, k2_ref, v0_ref, v1_ref, v2_ref, *slab) = refs
    else:
        (h_ref, cos_ref, sin_ref, ones_ref, ng_ref, win_ref, qg_ref, q0_ref, q1_ref, q2_ref, z_ref, *slab) = refs
    n = h_ref.shape[0]

    def deinterleave(val, slot, cs, dests):
        slab[slot][...] = val
        for ref, r in dests:
            for c in range(r):
                ref[c, :, cs] = slab[slot][pl.ds(c, n // r, stride=r), :].astype(BF16)

    h = h_ref[...]
    cos2 = cos_ref[...]
    sin2 = sin_ref[...]
    xn = _rms(h, ng_ref[...]).astype(BF16)
    qg = qg_ref[...]
    ones_bd = ones_ref[...]
    heads = [slice(hd * B_HD, (hd + 1) * B_HD) for hd in range(B_KV_HEADS)]

    pq = _head_rms(_dot(xn, win_ref[:, 2 * B_VW:3 * B_VW]), qg, ones_bd)
    for hd, cs in enumerate(heads):
        deinterleave(_rope(pq[:, cs], cos2, sin2), B_KV_HEADS + hd, cs, ((q2_ref, MAX_DIL),))
    if with_kv:
        xk = _rms(h, nkv_ref[...]).astype(BF16)
        pk = _head_rms(_dot(xk, wkv_ref[:, :B_VW]), kg_ref[...], ones_bd)
        for hd, cs in enumerate(heads):
            kh = _rope(pk[:, cs], cos2, sin2)
            k0_ref[:, cs] = kh.astype(BF16)
            deinterleave(kh, 2 * B_KV_HEADS + hd, cs, ((k1_ref, 4), (k2_ref, MAX_DIL)))
        pv = _dot(xk, wkv_ref[:, B_VW:])
        for hd, cs in enumerate(heads):
            v0_ref[:, cs] = pv[:, cs].astype(BF16)
            deinterleave(pv[:, cs], 3 * B_KV_HEADS + hd, cs, ((v1_ref, 4), (v2_ref, MAX_DIL)))
    pq = _head_rms(_dot(xn, win_ref[:, B_VW:2 * B_VW]), qg, ones_bd)
    for hd, cs in enumerate(heads):
        deinterleave(_rope(pq[:, cs], cos2, sin2), hd, cs, ((q1_ref, 4),))
    pq = _head_rms(_dot(xn, win_ref[:, :B_VW]), qg, ones_bd)
    for hd, cs in enumerate(heads):
        q0_ref[:, cs] = _rope(pq[:, cs], cos2, sin2).astype(BF16)
    z_ref[...] = _dot(xn, win_ref[:, 3 * B_VW:])


def _qkv_proj(h, cos2, sin2, norm_g, w_in, qnorm_g, kv_params=None):
    B, S, D = h.shape
    n = 512
    assert S % n == 0 and n % (8 * MAX_DIL) == 0
    with_kv = kv_params is not None

    def lay_specs():
        shapes, specs = [], []
        for _, r in DILATED_GROUPS:
            shapes.append(jax.ShapeDtypeStruct((B, r, S // r, B_VW), BF16))
            if r == 1:
                specs.append(pl.BlockSpec((None, None, n, B_VW), lambda b, i: (b, 0, i, 0)))
            else:
                specs.append(pl.BlockSpec((None, r, n // r, B_VW), lambda b, i: (b, 0, i, 0)))
        return shapes, specs

    row = lambda b, i: (b, i, 0)
    out_shape, out_specs = lay_specs()
    out_shape.append(jax.ShapeDtypeStruct((B, S, B_VW), F32))
    out_specs.append(pl.BlockSpec((None, n, B_VW), row))
    in_specs = [
        pl.BlockSpec((None, n, D), row),
        pl.BlockSpec((n, B_HD), lambda b, i: (i, 0)),
        pl.BlockSpec((n, B_HD), lambda b, i: (i, 0)),
        _const_spec((B_VW, B_VW)),
        _const_spec((1, D)), _const_spec((D, 4 * B_VW)), _const_spec((1, B_VW)),
    ]
    head_id = jnp.arange(B_VW) // B_HD
    ones_bd = (head_id[:, None] == head_id[None, :]).astype(BF16)
    tile4 = lambda g: jnp.tile(g, B_KV_HEADS)[None, :]
    args = [h, cos2, sin2, ones_bd, norm_g[None, :], w_in.astype(BF16), tile4(qnorm_g)]
    if with_kv:
        norm_kv, w_kv, knorm = kv_params
        in_specs += [_const_spec((1, D)), _const_spec((D, 2 * B_VW)), _const_spec((1, B_VW))]
        args += [norm_kv[None, :], w_kv.astype(BF16), tile4(knorm)]
        for _ in range(2):
            shp, sp = lay_specs()
            out_shape += shp
            out_specs += sp
    n_slots = (4 if with_kv else 2) * B_KV_HEADS
    outs = pl.pallas_call(
        functools.partial(_qkv_body, with_kv=with_kv),
        grid=(B, S // n),
        in_specs=in_specs,
        out_specs=out_specs,
        out_shape=out_shape,
        scratch_shapes=[pltpu.VMEM((n, LANES), F32) for _ in range(n_slots)],
        compiler_params=pltpu.CompilerParams(
            dimension_semantics=("arbitrary", "arbitrary"), vmem_limit_bytes=VMEM_LIMIT),
        name="qkv_proj_kv" if with_kv else "qkv_proj",
    )(*args)
    if with_kv:
        return outs[0:3], outs[3], outs[4:7], outs[7:10]
    return outs[0:3], outs[3]


def _attn_body(q_ref, kh_ref, kc_ref, vh_ref, vc_ref, o_ref, lse_ref, *scr, r):
    qblk = q_ref.shape[1]
    u0 = pl.program_id(1) * qblk
    nk = ATT_SUB + N_BACK
    qq = lax.broadcasted_iota(jnp.int32, (ATT_SUB, nk), 0)
    kk = lax.broadcasted_iota(jnp.int32, (ATT_SUB, nk), 1)
    dist = N_BACK + qq - kk
    band = (dist >= 0) & (dist <= N_BACK)
    lane = lax.broadcasted_iota(jnp.int32, (ATT_SUB, LANES), 1)
    scale = B_HD ** -0.5

    def residue(c):
        q = q_ref[c]
        k = jnp.concatenate([kh_ref[c], kc_ref[c]], axis=0)
        v = jnp.concatenate([vh_ref[c], vc_ref[c]], axis=0)
        for sb in range(qblk // ATT_SUB):
            r0 = sb * ATT_SUB
            kpos = u0 + (r0 - N_BACK) + kk
            bias = jnp.where(band & (kpos >= 0), 0.0, -jnp.inf)
            lse_blk = jnp.zeros((ATT_SUB, LANES), F32)
            for hd in range(B_KV_HEADS):
                cs = slice(hd * B_HD, (hd + 1) * B_HD)
                s = _dot_nt(q[r0:r0 + ATT_SUB, cs], k[r0:r0 + nk, cs]) * scale + bias
                mx = jnp.max(s, axis=1, keepdims=True)
                e = jnp.exp(s - mx)
                den = jnp.sum(e, axis=1, keepdims=True)
                o = _dot(e.astype(BF16), v[r0:r0 + nk, cs]) * (1.0 / den)
                lse_blk = jnp.where(lane == hd, mx + jnp.log(den), lse_blk)
                if r == 1:
                    o_ref[r0:r0 + ATT_SUB, cs] = o.astype(BF16)
                else:
                    scr[0][hd, pl.ds(r0 * r + c, ATT_SUB, stride=r), :] = o
            if r == 1:
                lse_ref[r0:r0 + ATT_SUB, :] = lse_blk
            else:
                scr[1][pl.ds(r0 * r + c, ATT_SUB, stride=r), :] = lse_blk

    if r <= 4:
        for c in range(r):
            residue(c)
    else:
        def loop_body(c, carry):
            residue(c)
            return carry
        lax.fori_loop(0, r, loop_body, 0, unroll=4)
    if r > 1:
        for hd in range(B_KV_HEADS):
            o_ref[:, hd * B_HD:(hd + 1) * B_HD] = scr[0][hd].astype(BF16)
        lse_ref[...] = scr[1][...]


def _window_attn(q, k, v):
    B, r, N, W = q.shape
    qblk = {1: 512, 4: 256}.get(r, ATT_SUB)
    assert N % qblk == 0
    per = qblk // N_BACK
    tt = qblk * r
    cur = lambda b, i: (b, 0, i, 0)
    halo = lambda b, i: (b, 0, jnp.maximum(i * per - 1, 0), 0)
    row = lambda b, i: (b, i, 0)
    scratch = [] if r == 1 else [pltpu.VMEM((B_KV_HEADS, tt, LANES), F32), pltpu.VMEM((tt, LANES), F32)]
    return pl.pallas_call(
        functools.partial(_attn_body, r=r),
        grid=(B, N // qblk),
        in_specs=[
            pl.BlockSpec((None, r, qblk, W), cur),
            pl.BlockSpec((None, r, N_BACK, W), halo),
            pl.BlockSpec((None, r, qblk, W), cur),
            pl.BlockSpec((None, r, N_BACK, W), halo),
            pl.BlockSpec((None, r, qblk, W), cur),
        ],
        out_specs=[pl.BlockSpec((None, tt, W), row), pl.BlockSpec((None, tt, LANES), row)],
        out_shape=[
            jax.ShapeDtypeStruct((B, N * r, W), BF16),
            jax.ShapeDtypeStruct((B, N * r, LANES), F32),
        ],
        scratch_shapes=scratch,
        compiler_params=pltpu.CompilerParams(
            dimension_semantics=("arbitrary", "arbitrary"), vmem_limit_bytes=VMEM_LIMIT),
        name=f"window_attn_r{r}",
    )(q, k, k, v, v)


def _post_body(h_ref, p_ref, z_ref, o0_ref, o1_ref, o2_ref, l0_ref, l1_ref, l2_ref,
               wout_ref, png_ref, wpg_ref, wpp_ref, out_ref, y_scr, ynext_scr):
    @pl.when(pl.program_id(0) == 0)
    def _():
        y_scr[...] = jnp.zeros(y_scr.shape, BF16)

    half = h_ref.shape[0] // 2
    halves = [slice(r0, r0 + half) for r0 in (0, half)]
    hns = [h_ref[rs, :] + _dot(y_scr[rs, :], wout_ref[...]) for rs in halves]
    for rs, hn in zip(halves, hns):
        gate = _sigmoid(_dot(_rms(hn, png_ref[...]).astype(BF16), wpg_ref[...]))
        emb = _dot(p_ref[rs, :].astype(BF16), wpp_ref[...])
        out_ref[rs, :] = hn + emb * gate

    l0, l1, l2 = l0_ref[...], l1_ref[...], l2_ref[...]
    mx = jnp.maximum(jnp.maximum(l0, l1), l2)
    e0, e1, e2 = jnp.exp(l0 - mx), jnp.exp(l1 - mx), jnp.exp(l2 - mx)
    tot = e0 + e1 + e2
    w0, w1, w2 = e0 / tot, e1 / tot, e2 / tot
    for hd in range(B_KV_HEADS):
        cs = slice(hd * B_HD, (hd + 1) * B_HD)
        o = (w0[:, hd:hd + 1] * o0_ref[:, cs].astype(F32) + w1[:, hd:hd + 1] * o1_ref[:, cs].astype(F32)
             + w2[:, hd:hd + 1] * o2_ref[:, cs].astype(F32))
        ynext_scr[:, cs] = (o * _silu(z_ref[:, cs])).astype(BF16)
    y_scr[...] = ynext_scr[...]


def _attn_post(h, p, layer, z, os_, lses, w_out, ple_norm, w_pg, w_pp):
    B, S, D = h.shape
    P = p.shape[-1]
    n = 512
    assert S % n == 0
    per = S // n
    last = B * per - 1
    nxt = lambda t: (jnp.minimum(t, last) // per, jnp.minimum(t, last) % per, 0)
    prv = lambda t: (jnp.maximum(t - 1, 0) // per, jnp.maximum(t - 1, 0) % per, 0)
    in_specs = [
        pl.BlockSpec((None, n, D), prv),
        pl.BlockSpec((None, None, n, P), lambda t: (layer,) + prv(t)),
        pl.BlockSpec((None, n, B_VW), nxt),
        pl.BlockSpec((None, n, B_VW), nxt), pl.BlockSpec((None, n, B_VW), nxt), pl.BlockSpec((None, n, B_VW), nxt),
        pl.BlockSpec((None, n, LANES), nxt), pl.BlockSpec((None, n, LANES), nxt), pl.BlockSpec((None, n, LANES), nxt),
        _const_spec((B_VW, D)), _const_spec((1, D)), _const_spec((D, D)), _const_spec((P, D)),
    ]
    return pl.pallas_call(
        _post_body,
        grid=(B * per + 1,),
        in_specs=in_specs,
        out_specs=pl.BlockSpec((None, n, D), prv),
        out_shape=jax.ShapeDtypeStruct((B, S, D), F32),
        scratch_shapes=[pltpu.VMEM((n, B_VW), BF16), pltpu.VMEM((n, B_VW), BF16)],
        compiler_params=pltpu.CompilerParams(
            dimension_semantics=("arbitrary",), vmem_limit_bytes=VMEM_LIMIT),
        name="attn_post",
    )(h, p, z, *os_, *lses, w_out.astype(BF16), ple_norm[None, :], w_pg.astype(BF16), w_pp.astype(BF16))


def _rope_tables(seq):
    inv = ROPE_THETA ** (-jnp.arange(0, B_HD, 2, dtype=F32) / B_HD)
    ang = jnp.arange(seq, dtype=F32)[:, None] * inv[None, :]
    cos, sin = jnp.cos(ang), jnp.sin(ang)
    return jnp.concatenate([cos, cos], axis=1), jnp.concatenate([-sin, sin], axis=1)


def kernel(x, p, norm_a, w_in_a, conv_a, b_gate_a, hnorm_a, w_out_a, norm_kv, w_kv, knorm, norm_b, w_in_b, qnorm_b, w_out_b, ple_norm, w_ple_gate, w_ple):
    S = x.shape[1]
    n_a = norm_a.shape[0]
    n_b = norm_b.shape[0]
    assert all(w // d == N_BACK for w, d in DILATED_GROUPS)
    cos2, sin2 = _rope_tables(S)
    h = x
    for l in range(n_a):
        h = _mlstm_layer(h, p, l, norm_a[l], w_in_a[l], conv_a[l], b_gate_a[l], hnorm_a[l], w_out_a[l],
                         ple_norm[l], w_ple_gate[l], w_ple[l])
    ks = vs = None
    for j in range(n_b):
        l = n_a + j
        if j == 0:
            qs, z, ks, vs = _qkv_proj(h, cos2, sin2, norm_b[j], w_in_b[j], qnorm_b[j], (norm_kv, w_kv, knorm))
        else:
            qs, z = _qkv_proj(h, cos2, sin2, norm_b[j], w_in_b[j], qnorm_b[j])
        res = [_window_attn(qs[g], ks[g], vs[g]) for g in range(len(DILATED_GROUPS))]
        h = _attn_post(h, p, l, z, [r[0] for r in res], [r[1] for r in res],
                       w_out_b[j], ple_norm[l], w_ple_gate[l], w_ple[l])
    return h
```

```python
import functools

import jax
import jax.numpy as jnp
from jax import lax
from jax.experimental import pallas as pl
from jax.experimental.pallas import tpu as pltpu

F32 = jnp.float32
BF16 = jnp.bfloat16

EPS = 1e-6
ROPE_THETA = 10000.0

A_HEADS = 4
A_DQK = 128
A_DV = 256
A_QK = A_HEADS * A_DQK
A_VW = A_HEADS * A_DV
A_CONV = 4
MLSTM_CHUNK = 128
MLSTM_TILE = 512

B_KV_HEADS = 4
B_HD = 128
B_VW = B_KV_HEADS * B_HD
DILATED_GROUPS = ((128, 1), (512, 4), (2048, 16))
N_BACK = 128
MAX_DIL = 16
ATT_SUB = 128

LANES = 128
VMEM_LIMIT = 56 * 1024 * 1024


def _dot(a, b):
    return jnp.dot(a, b, preferred_element_type=F32)


def _dot_nt(a, b):
    return lax.dot_general(a, b, (((1,), (1,)), ((), ())), preferred_element_type=F32)


def _rms(x, g):
    ms = jnp.mean(x * x, axis=-1, keepdims=True)
    return x * lax.rsqrt(ms + EPS) * g


def _sigmoid(x):
    return 1.0 / (1.0 + jnp.exp(-x))


def _silu(x):
    return x * _sigmoid(x)


def _log_sigmoid(x):
    return jnp.minimum(x, 0.0) - jnp.log1p(jnp.exp(-jnp.abs(x)))


def _const_spec(shape):
    nd = len(shape)
    return pl.BlockSpec(shape, lambda *_: (0,) * nd, pipeline_mode=pl.Buffered(1))


def _mlstm_body(h_ref, p_ref, ng_ref, wqk_ref, wv_ref, wo_ref, wz_ref, wg_ref, conv_ref, bg_ref,
                hng_ref, wout_ref, png_ref, wpg_ref, wpp_ref, out_ref,
                qk_scr, hh_scr, ct_scr, n_scr, m_scr):
    T = h_ref.shape[0]
    L = MLSTM_CHUNK
    step = pl.program_id(1)

    @pl.when(step == 0)
    def _():
        qk_scr[0:8, :] = jnp.zeros((8, 2 * A_QK), F32)
        ct_scr[...] = jnp.zeros(ct_scr.shape, F32)
        n_scr[...] = jnp.zeros(n_scr.shape, F32)
        m_scr[...] = jnp.zeros(m_scr.shape, F32)

    h = h_ref[...]
    xn = _rms(h, ng_ref[...]).astype(BF16)
    g2 = _dot(xn, wg_ref[...]) + bg_ref[...]
    gi = g2[:, :LANES]
    gf = g2[:, LANES:]

    row_i = lax.broadcasted_iota(jnp.int32, (L, L), 0)
    col_i = lax.broadcasted_iota(jnp.int32, (L, L), 1)
    causal = col_i <= row_i
    tril = jnp.where(causal, 1.0, 0.0).astype(BF16)
    lf = _log_sigmoid(gf)
    lf1 = lf.astype(BF16)
    r1 = lf - lf1.astype(F32)
    lf2 = r1.astype(BF16)
    lf3 = (r1 - lf2.astype(F32)).astype(BF16)
    b_col = jnp.concatenate(
        [_dot(tril, lf1[c * L:(c + 1) * L]) + _dot(tril, lf2[c * L:(c + 1) * L]) + _dot(tril, lf3[c * L:(c + 1) * L])
         for c in range(T // L)], axis=0)
    bb_col = gi - b_col
    bb_row = bb_col.T[0:8, :]
    m_old = m_scr[...]

    qk_pre = _dot(xn, wqk_ref[...])
    v = _dot(xn, wv_ref[...])
    o_pre = _dot(xn, wo_ref[...])
    z = _dot(xn, wz_ref[...])

    qk_scr[8:8 + T, :] = qk_pre
    cw = conv_ref[...]
    conv = (qk_scr[5:5 + T, :] * cw[0:1, :] + qk_scr[6:6 + T, :] * cw[1:2, :]
            + qk_scr[7:7 + T, :] * cw[2:3, :] + qk_pre * cw[3:4, :])
    qk_scr[0:8, :] = qk_scr[T:T + 8, :]
    qk = _silu(conv)
    q = qk[:, :A_QK]
    k = qk[:, A_QK:] * (A_DQK ** -0.5)

    for hd in range(A_HEADS):
        m_h = m_old[hd:hd + 1, 0:1]
        ct = ct_scr[hd]
        nrow = n_scr[hd:hd + 1, :]
        for c in range(T // L):
            rs = slice(c * L, (c + 1) * L)
            qh = q[rs, hd * A_DQK:(hd + 1) * A_DQK]
            kh = k[rs, hd * A_DQK:(hd + 1) * A_DQK]
            vh = v[rs, hd * A_DV:(hd + 1) * A_DV]
            qb = qh.astype(BF16)
            br = bb_row[hd:hd + 1, rs]
            t = jnp.where(causal, br, -jnp.inf)
            mc = jnp.maximum(jnp.max(t, axis=1, keepdims=True), m_h)
            m_last = jnp.maximum(jnp.max(br, axis=1, keepdims=True), m_h)
            b_last = b_col[(c + 1) * L - 1:(c + 1) * L, hd:hd + 1]

            w = jnp.exp(t - mc) * _dot_nt(qb, kh.astype(BF16))
            inter = jnp.exp(m_h - mc)
            num = _dot(w.astype(BF16), vh.astype(BF16)) + inter * _dot(qb, ct.astype(BF16))
            den = jnp.sum(w, axis=1, keepdims=True) + inter * jnp.sum(qh * nrow, axis=1, keepdims=True)
            floor = jnp.exp(-(b_col[rs, hd:hd + 1] + mc))
            hh_scr[rs, hd * A_DV:(hd + 1) * A_DV] = num * (1.0 / jnp.maximum(jnp.abs(den), floor))

            wj = jnp.exp(bb_col[rs, hd:hd + 1] - m_last)
            decay = jnp.exp(m_h - m_last)
            ct = decay * ct + _dot(kh.T.astype(BF16), (wj * vh).astype(BF16))
            nrow = decay * nrow + jnp.sum(wj * kh, axis=0, keepdims=True)
            m_h = b_last + m_last
        ct_scr[hd] = ct
        n_scr[hd:hd + 1, :] = nrow
        m_scr[hd:hd + 1, :] = jnp.broadcast_to(m_h, (1, LANES))

    hng = hng_ref[...]
    half = T // 2
    for r0 in (0, half):
        rs = slice(r0, r0 + half)
        og = _sigmoid(o_pre[rs])
        ys = []
        for hd in range(A_HEADS):
            sl = slice(hd * A_DV, (hd + 1) * A_DV)
            ht = og[:, sl] * hh_scr[rs, sl]
            ys.append(_rms(ht, hng[:, sl]) * _silu(z[rs, sl]))
        y = jnp.concatenate(ys, axis=1).astype(BF16)
        hn = h[rs] + _dot(y, wout_ref[...])
        gate = _sigmoid(_dot(_rms(hn, png_ref[...]).astype(BF16), wpg_ref[...]))
        emb = _dot(p_ref[rs, :].astype(BF16), wpp_ref[...])
        out_ref[rs, :] = hn + emb * gate


def _mlstm_layer(h, p, layer, norm_g, w_in, conv_w, b_gate, hnorm_g, w_out, ple_norm, w_pg, w_pp):
    B, S, D = h.shape
    L = MLSTM_TILE
    assert S % L == 0 and L % (2 * MLSTM_CHUNK) == 0
    P = p.shape[-1]
    wb = w_in.astype(BF16)
    o0 = 2 * A_QK
    wqk, wv, wo, wz = wb[:, :o0], wb[:, o0:o0 + A_VW], wb[:, o0 + A_VW:o0 + 2 * A_VW], wb[:, o0 + 2 * A_VW:o0 + 3 * A_VW]
    wgi = wb[:, o0 + 3 * A_VW:o0 + 3 * A_VW + A_HEADS]
    wgf = wb[:, o0 + 3 * A_VW + A_HEADS:]
    pad = jnp.zeros((D, LANES - A_HEADS), BF16)
    wg = jnp.concatenate([wgi, pad, wgf, pad], axis=1)
    bpad = jnp.zeros((LANES - A_HEADS,), F32)
    bg = jnp.concatenate([b_gate[:A_HEADS], bpad, b_gate[A_HEADS:], bpad])[None, :]

    row = lambda b, i: (b, i, 0)
    in_specs = [
        pl.BlockSpec((None, L, D), row),
        pl.BlockSpec((None, None, L, P), lambda b, i: (layer, b, i, 0)),
        _const_spec((1, D)),
        _const_spec((D, 2 * A_QK)), _const_spec((D, A_VW)), _const_spec((D, A_VW)), _const_spec((D, A_VW)),
        _const_spec((D, 2 * LANES)),
        _const_spec((A_CONV, 2 * A_QK)), _const_spec((1, 2 * LANES)), _const_spec((1, A_VW)),
        _const_spec((A_VW, D)), _const_spec((1, D)), _const_spec((D, D)), _const_spec((P, D)),
    ]
    return pl.pallas_call(
        _mlstm_body,
        grid=(B, S // L),
        in_specs=in_specs,
        out_specs=pl.BlockSpec((None, L, D), row),
        out_shape=jax.ShapeDtypeStruct((B, S, D), F32),
        scratch_shapes=[
            pltpu.VMEM((L + 8, 2 * A_QK), F32),
            pltpu.VMEM((L, A_VW), F32),
            pltpu.VMEM((A_HEADS, A_DQK, A_DV), F32),
            pltpu.VMEM((8, A_DQK), F32),
            pltpu.VMEM((8, LANES), F32),
        ],
        compiler_params=pltpu.CompilerParams(
            dimension_semantics=("arbitrary", "arbitrary"), vmem_limit_bytes=VMEM_LIMIT),
        name="mlstm_layer",
    )(h, p, norm_g[None, :], wqk, wv, wo, wz, wg, conv_w, bg, hnorm_g[None, :],
      w_out.astype(BF16), ple_norm[None, :], w_pg.astype(BF16), w_pp.astype(BF16))


def _head_rms(x, g4, ones_bd):
    ss = _dot((x * x).astype(BF16), ones_bd)
    return x * lax.rsqrt(ss * (1.0 / B_HD) + EPS) * g4


def _rope(xn, cos2, sin2):
    return xn * cos2 + pltpu.roll(xn, B_HD // 2, axis=1) * sin2


def _qkv_body(*refs, with_kv):
    if with_kv:
        (h_ref, cos_ref, sin_ref, ones_ref, ng_ref, win_ref, qg_ref, nkv_ref, wkv_ref, kg_ref,
         q0_ref, q1_ref, q2_ref, z_ref, k0_ref, k1_ref, k2_ref, v0_ref, v1_ref, v2_ref, *slab) = refs
    else:
        (h_ref, cos_ref, sin_ref, ones_ref, ng_ref, win_ref, qg_ref, q0_ref, q1_ref, q2_ref, z_ref, *slab) = refs
    n = h_ref.shape[0]

    def deinterleave(val, slot, cs, dests):
        slab[slot][...] = val
        for ref, r in dests:
            for c in range(r):
                ref[c, :, cs] = slab[slot][pl.ds(c, n // r, stride=r), :].astype(BF16)

    h = h_ref[...]
    cos2 = cos_ref[...]
    sin2 = sin_ref[...]
    xn = _rms(h, ng_ref[...]).astype(BF16)
    qg = qg_ref[...]
    ones_bd = ones_ref[...]
    heads = [slice(hd * B_HD, (hd + 1) * B_HD) for hd in range(B_KV_HEADS)]

    pq = _head_rms(_dot(xn, win_ref[:, 2 * B_VW:3 * B_VW]), qg, ones_bd)
    for hd, cs in enumerate(heads):
        deinterleave(_rope(pq[:, cs], cos2, sin2), B_KV_HEADS + hd, cs, ((q2_ref, MAX_DIL),))
    if with_kv:
        xk = _rms(h, nkv_ref[...]).astype(BF16)
        pk = _head_rms(_dot(xk, wkv_ref[:, :B_VW]), kg_ref[...], ones_bd)
        for hd, cs in enumerate(heads):
            kh = _rope(pk[:, cs], cos2, sin2)
            k0_ref[:, cs] = kh.astype(BF16)
            deinterleave(kh, 2 * B_KV_HEADS + hd, cs, ((k1_ref, 4), (k2_ref, MAX_DIL)))
        pv = _dot(xk, wkv_ref[:, B_VW:])
        for hd, cs in enumerate(heads):
            v0_ref[:, cs] = pv[:, cs].astype(BF16)
            deinterleave(pv[:, cs], 3 * B_KV_HEADS + hd, cs, ((v1_ref, 4), (v2_ref, MAX_DIL)))
    pq = _head_rms(_dot(xn, win_ref[:, B_VW:2 * B_VW]), qg, ones_bd)
    for hd, cs in enumerate(heads):
        deinterleave(_rope(pq[:, cs], cos2, sin2), hd, cs, ((q1_ref, 4),))
    pq = _head_rms(_dot(xn, win_ref[:, :B_VW]), qg, ones_bd)
    for hd, cs in enumerate(heads):
        q0_ref[:, cs] = _rope(pq[:, cs], cos2, sin2).astype(BF16)
    z_ref[...] = _dot(xn, win_ref[:, 3 * B_VW:]).astype(BF16)


def _qkv_proj(h, cos2, sin2, norm_g, w_in, qnorm_g, kv_params=None):
    B, S, D = h.shape
    n = 512
    assert S % n == 0 and n % (8 * MAX_DIL) == 0
    with_kv = kv_params is not None

    def lay_specs():
        shapes, specs = [], []
        for _, r in DILATED_GROUPS:
            shapes.append(jax.ShapeDtypeStruct((B, r, S // r, B_VW), BF16))
            if r == 1:
                specs.append(pl.BlockSpec((None, None, n, B_VW), lambda b, i: (b, 0, i, 0)))
            else:
                specs.append(pl.BlockSpec((None, r, n // r, B_VW), lambda b, i: (b, 0, i, 0)))
        return shapes, specs

    row = lambda b, i: (b, i, 0)
    out_shape, out_specs = lay_specs()
    out_shape.append(jax.ShapeDtypeStruct((B, S, B_VW), BF16))
    out_specs.append(pl.BlockSpec((None, n, B_VW), row))
    in_specs = [
        pl.BlockSpec((None, n, D), row),
        pl.BlockSpec((n, B_HD), lambda b, i: (i, 0)),
        pl.BlockSpec((n, B_HD), lambda b, i: (i, 0)),
        _const_spec((B_VW, B_VW)),
        _const_spec((1, D)), _const_spec((D, 4 * B_VW)), _const_spec((1, B_VW)),
    ]
    head_id = jnp.arange(B_VW) // B_HD
    ones_bd = (head_id[:, None] == head_id[None, :]).astype(BF16)
    tile4 = lambda g: jnp.tile(g, B_KV_HEADS)[None, :]
    args = [h, cos2, sin2, ones_bd, norm_g[None, :], w_in.astype(BF16), tile4(qnorm_g)]
    if with_kv:
        norm_kv, w_kv, knorm = kv_params
        in_specs += [_const_spec((1, D)), _const_spec((D, 2 * B_VW)), _const_spec((1, B_VW))]
        args += [norm_kv[None, :], w_kv.astype(BF16), tile4(knorm)]
        for _ in range(2):
            shp, sp = lay_specs()
            out_shape += shp
            out_specs += sp
    n_slots = (4 if with_kv else 2) * B_KV_HEADS
    outs = pl.pallas_call(
        functools.partial(_qkv_body, with_kv=with_kv),
        grid=(B, S // n),
        in_specs=in_specs,
        out_specs=out_specs,
        out_shape=out_shape,
        scratch_shapes=[pltpu.VMEM((n, LANES), F32) for _ in range(n_slots)],
        compiler_params=pltpu.CompilerParams(
            dimension_semantics=("arbitrary", "arbitrary"), vmem_limit_bytes=VMEM_LIMIT),
        name="qkv_proj_kv" if with_kv else "qkv_proj",
    )(*args)
    if with_kv:
        return outs[0:3], outs[3], outs[4:7], outs[7:10]
    return outs[0:3], outs[3]


def _attn_body(q_ref, kh_ref, kc_ref, vh_ref, vc_ref, o_ref, lse_ref, *scr, r):
    qblk = q_ref.shape[1]
    u0 = pl.program_id(1) * qblk
    nk = ATT_SUB + N_BACK
    qq = lax.broadcasted_iota(jnp.int32, (ATT_SUB, nk), 0)
    kk = lax.broadcasted_iota(jnp.int32, (ATT_SUB, nk), 1)
    dist = N_BACK + qq - kk
    band = (dist >= 0) & (dist <= N_BACK)
    lane = lax.broadcasted_iota(jnp.int32, (ATT_SUB, LANES), 1)
    scale = B_HD ** -0.5

    def residue(c):
        q = q_ref[c]
        k = jnp.concatenate([kh_ref[c], kc_ref[c]], axis=0)
        v = jnp.concatenate([vh_ref[c], vc_ref[c]], axis=0)
        for sb in range(qblk // ATT_SUB):
            r0 = sb * ATT_SUB
            kpos = u0 + (r0 - N_BACK) + kk
            bias = jnp.where(band & (kpos >= 0), 0.0, -jnp.inf)
            lse_blk = jnp.zeros((ATT_SUB, LANES), F32)
            for hd in range(B_KV_HEADS):
                cs = slice(hd * B_HD, (hd + 1) * B_HD)
                s = _dot_nt(q[r0:r0 + ATT_SUB, cs], k[r0:r0 + nk, cs]) * scale + bias
                mx = jnp.max(s, axis=1, keepdims=True)
                e = jnp.exp(s - mx)
                den = jnp.sum(e, axis=1, keepdims=True)
                o = _dot(e.astype(BF16), v[r0:r0 + nk, cs]) * (1.0 / den)
                lse_blk = jnp.where(lane == hd, mx + jnp.log(den), lse_blk)
                if r == 1:
                    o_ref[r0:r0 + ATT_SUB, cs] = o.astype(BF16)
                else:
                    scr[0][hd, pl.ds(r0 * r + c, ATT_SUB, stride=r), :] = o
            if r == 1:
                lse_ref[r0:r0 + ATT_SUB, :] = lse_blk
            else:
                scr[1][pl.ds(r0 * r + c, ATT_SUB, stride=r), :] = lse_blk

    if r <= 4:
        for c in range(r):
            residue(c)
    else:
        def loop_body(c, carry):
            residue(c)
            return carry
        lax.fori_loop(0, r, loop_body, 0, unroll=True)
    if r > 1:
        for hd in range(B_KV_HEADS):
            o_ref[:, hd * B_HD:(hd + 1) * B_HD] = scr[0][hd].astype(BF16)
        lse_ref[...] = scr[1][...]


def _window_attn(q, k, v):
    B, r, N, W = q.shape
    qblk = {1: 2048, 4: 512}.get(r, ATT_SUB)
    assert N % qblk == 0
    per = qblk // N_BACK
    tt = qblk * r
    cur = lambda b, i: (b, 0, i, 0)
    halo = lambda b, i: (b, 0, jnp.maximum(i * per - 1, 0), 0)
    row = lambda b, i: (b, i, 0)
    scratch = [] if r == 1 else [pltpu.VMEM((B_KV_HEADS, tt, LANES), F32), pltpu.VMEM((tt, LANES), F32)]
    return pl.pallas_call(
        functools.partial(_attn_body, r=r),
        grid=(B, N // qblk),
        in_specs=[
            pl.BlockSpec((None, r, qblk, W), cur),
            pl.BlockSpec((None, r, N_BACK, W), halo),
            pl.BlockSpec((None, r, qblk, W), cur),
            pl.BlockSpec((None, r, N_BACK, W), halo),
            pl.BlockSpec((None, r, qblk, W), cur),
        ],
        out_specs=[pl.BlockSpec((None, tt, W), row), pl.BlockSpec((None, tt, LANES), row)],
        out_shape=[
            jax.ShapeDtypeStruct((B, N * r, W), BF16),
            jax.ShapeDtypeStruct((B, N * r, LANES), F32),
        ],
        scratch_shapes=scratch,
        compiler_params=pltpu.CompilerParams(
            dimension_semantics=("arbitrary", "arbitrary"), vmem_limit_bytes=VMEM_LIMIT),
        name=f"window_attn_r{r}",
    )(q, k, k, v, v)


def _post_body(h_ref, p_ref, z_ref, o0_ref, o1_ref, o2_ref, l0_ref, l1_ref, l2_ref,
               wout_ref, png_ref, wpg_ref, wpp_ref, out_ref, y_scr, ynext_scr):
    @pl.when(pl.program_id(0) == 0)
    def _():
        y_scr[...] = jnp.zeros(y_scr.shape, BF16)

    half = h_ref.shape[0] // 2
    halves = [slice(r0, r0 + half) for r0 in (0, half)]
    hns = [h_ref[rs, :] + _dot(y_scr[rs, :], wout_ref[...]) for rs in halves]
    for rs, hn in zip(halves, hns):
        gate = _sigmoid(_dot(_rms(hn, png_ref[...]).astype(BF16), wpg_ref[...]))
        emb = _dot(p_ref[rs, :].astype(BF16), wpp_ref[...])
        out_ref[rs, :] = hn + emb * gate

    l0, l1, l2 = l0_ref[...], l1_ref[...], l2_ref[...]
    mx = jnp.maximum(jnp.maximum(l0, l1), l2)
    e0, e1, e2 = jnp.exp(l0 - mx), jnp.exp(l1 - mx), jnp.exp(l2 - mx)
    tot = e0 + e1 + e2
    w0, w1, w2 = e0 / tot, e1 / tot, e2 / tot
    for hd in range(B_KV_HEADS):
        cs = slice(hd * B_HD, (hd + 1) * B_HD)
        o = (w0[:, hd:hd + 1] * o0_ref[:, cs].astype(F32) + w1[:, hd:hd + 1] * o1_ref[:, cs].astype(F32)
             + w2[:, hd:hd + 1] * o2_ref[:, cs].astype(F32))
        ynext_scr[:, cs] = (o * _silu(z_ref[:, cs].astype(F32))).astype(BF16)
    y_scr[...] = ynext_scr[...]


def _attn_post(h, p, layer, z, os_, lses, w_out, ple_norm, w_pg, w_pp):
    B, S, D = h.shape
    P = p.shape[-1]
    n = 512
    assert S % n == 0
    per = S // n
    last = B * per - 1
    nxt = lambda t: (jnp.minimum(t, last) // per, jnp.minimum(t, last) % per, 0)
    prv = lambda t: (jnp.maximum(t - 1, 0) // per, jnp.maximum(t - 1, 0) % per, 0)
    in_specs = [
        pl.BlockSpec((None, n, D), prv),
        pl.BlockSpec((None, None, n, P), lambda t: (layer,) + prv(t)),
        pl.BlockSpec((None, n, B_VW), nxt),
        pl.BlockSpec((None, n, B_VW), nxt), pl.BlockSpec((None, n, B_VW), nxt), pl.BlockSpec((None, n, B_VW), nxt),
        pl.BlockSpec((None, n, LANES), nxt), pl.BlockSpec((None, n, LANES), nxt), pl.BlockSpec((None, n, LANES), nxt),
        _const_spec((B_VW, D)), _const_spec((1, D)), _const_spec((D, D)), _const_spec((P, D)),
    ]
    return pl.pallas_call(
        _post_body,
        grid=(B * per + 1,),
        in_specs=in_specs,
        out_specs=pl.BlockSpec((None, n, D), prv),
        out_shape=jax.ShapeDtypeStruct((B, S, D), F32),
        scratch_shapes=[pltpu.VMEM((n, B_VW), BF16), pltpu.VMEM((n, B_VW), BF16)],
        compiler_params=pltpu.CompilerParams(
            dimension_semantics=("arbitrary",), vmem_limit_bytes=VMEM_LIMIT),
        name="attn_post",
    )(h, p, z, *os_, *lses, w_out.astype(BF16), ple_norm[None, :], w_pg.astype(BF16), w_pp.astype(BF16))


def _rope_tables(seq):
    inv = ROPE_THETA ** (-jnp.arange(0, B_HD, 2, dtype=F32) / B_HD)
    ang = jnp.arange(seq, dtype=F32)[:, None] * inv[None, :]
    cos, sin = jnp.cos(ang), jnp.sin(ang)
    return jnp.concatenate([cos, cos], axis=1), jnp.concatenate([-sin, sin], axis=1)


def kernel(x, p, norm_a, w_in_a, conv_a, b_gate_a, hnorm_a, w_out_a, norm_kv, w_kv, knorm, norm_b, w_in_b, qnorm_b, w_out_b, ple_norm, w_ple_gate, w_ple):
    S = x.shape[1]
    n_a = norm_a.shape[0]
    n_b = norm_b.shape[0]
    assert all(w // d == N_BACK for w, d in DILATED_GROUPS)
    cos2, sin2 = _rope_tables(S)
    h = x
    for l in range(n_a):
        h = _mlstm_layer(h, p, l, norm_a[l], w_in_a[l], conv_a[l], b_gate_a[l], hnorm_a[l], w_out_a[l],
                         ple_norm[l], w_ple_gate[l], w_ple[l])
    ks = vs = None
    for j in range(n_b):
        l = n_a + j
        if j == 0:
            qs, z, ks, vs = _qkv_proj(h, cos2, sin2, norm_b[j], w_in_b[j], qnorm_b[j], (norm_kv, w_kv, knorm))
        else:
            qs, z = _qkv_proj(h, cos2, sin2, norm_b[j], w_in_b[j], qnorm_b[j])
        res = [_window_attn(qs[g], ks[g], vs[g]) for g in range(len(DILATED_GROUPS))]
        h = _attn_post(h, p, l, z, [r[0] for r in res], [r[1] for r in res],
                       w_out_b[j], ple_norm[l], w_ple_gate[l], w_ple[l])
    return h
```

```python
import functools

import jax
import jax.numpy as jnp
from jax import lax
from jax.experimental import pallas as pl
from jax.experimental.pallas import tpu as pltpu

F32 = jnp.float32
BF16 = jnp.bfloat16

EPS = 1e-6
ROPE_THETA = 10000.0

A_HEADS = 4
A_DQK = 128
A_DV = 256
A_QK = A_HEADS * A_DQK
A_VW = A_HEADS * A_DV
A_CONV = 4
MLSTM_CHUNK = 128
MLSTM_TILE = 512

B_KV_HEADS = 4
B_HD = 128
B_VW = B_KV_HEADS * B_HD
DILATED_GROUPS = ((128, 1), (512, 4), (2048, 16))
N_BACK = 128
MAX_DIL = 16
ATT_SUB = 128

LANES = 128
VMEM_LIMIT = 56 * 1024 * 1024


def _dot(a, b):
    return jnp.dot(a, b, preferred_element_type=F32)


def _dot_nt(a, b):
    return lax.dot_general(a, b, (((1,), (1,)), ((), ())), preferred_element_type=F32)


def _rms(x, g):
    ms = jnp.mean(x * x, axis=-1, keepdims=True)
    return x * lax.rsqrt(ms + EPS) * g


def _sigmoid(x):
    return 1.0 / (1.0 + jnp.exp(-x))


def _silu(x):
    return x * _sigmoid(x)


def _log_sigmoid(x):
    return jnp.minimum(x, 0.0) - jnp.log1p(jnp.exp(-jnp.abs(x)))


def _const_spec(shape):
    nd = len(shape)
    return pl.BlockSpec(shape, lambda *_: (0,) * nd, pipeline_mode=pl.Buffered(1))


def _mlstm_body(h_ref, hprev_ref, pprev_ref, ng_ref, wqk_ref, wv_ref, wo_ref, wz_ref, wg_ref, conv_ref, bg_ref,
                hng_ref, wout_ref, png_ref, wpg_ref, wpp_ref, out_ref,
                qk_scr, hh_scr, ct_scr, n_scr, m_scr, y_scr, ynext_scr, *, tiles_per_seq):
    T = h_ref.shape[0]
    L = MLSTM_CHUNK
    t = pl.program_id(0)

    @pl.when(t == 0)
    def _():
        y_scr[...] = jnp.zeros(y_scr.shape, BF16)

    @pl.when(t % tiles_per_seq == 0)
    def _():
        qk_scr[0:8, :] = jnp.zeros((8, 2 * A_QK), F32)
        ct_scr[...] = jnp.zeros(ct_scr.shape, F32)
        n_scr[...] = jnp.zeros(n_scr.shape, F32)
        m_scr[...] = jnp.zeros(m_scr.shape, F32)

    half = T // 2
    halves = [slice(r0, r0 + half) for r0 in (0, half)]
    hns = [hprev_ref[rs, :] + _dot(y_scr[rs, :], wout_ref[...]) for rs in halves]
    for rs, hn in zip(halves, hns):
        gate = _sigmoid(_dot(_rms(hn, png_ref[...]).astype(BF16), wpg_ref[...]))
        emb = _dot(pprev_ref[rs, :].astype(BF16), wpp_ref[...])
        out_ref[rs, :] = hn + emb * gate

    h = h_ref[...]
    xn = _rms(h, ng_ref[...]).astype(BF16)
    g2 = _dot(xn, wg_ref[...]) + bg_ref[...]
    gi = g2[:, :LANES]
    gf = g2[:, LANES:]

    row_i = lax.broadcasted_iota(jnp.int32, (L, L), 0)
    col_i = lax.broadcasted_iota(jnp.int32, (L, L), 1)
    causal = col_i <= row_i
    tril = jnp.where(causal, 1.0, 0.0).astype(BF16)
    lf = _log_sigmoid(gf)
    lf1 = lf.astype(BF16)
    r1 = lf - lf1.astype(F32)
    lf2 = r1.astype(BF16)
    lf3 = (r1 - lf2.astype(F32)).astype(BF16)
    b_col = jnp.concatenate(
        [_dot(tril, lf1[c * L:(c + 1) * L]) + _dot(tril, lf2[c * L:(c + 1) * L]) + _dot(tril, lf3[c * L:(c + 1) * L])
         for c in range(T // L)], axis=0)
    bb_col = gi - b_col
    bb_row = bb_col.T[0:8, :]
    m_old = m_scr[...]

    qk_pre = _dot(xn, wqk_ref[...])
    v = _dot(xn, wv_ref[...])
    o_pre = _dot(xn, wo_ref[...])
    z = _dot(xn, wz_ref[...])

    qk_scr[8:8 + T, :] = qk_pre
    cw = conv_ref[...]
    conv = (qk_scr[5:5 + T, :] * cw[0:1, :] + qk_scr[6:6 + T, :] * cw[1:2, :]
            + qk_scr[7:7 + T, :] * cw[2:3, :] + qk_pre * cw[3:4, :])
    qk_scr[0:8, :] = qk_scr[T:T + 8, :]
    qk = _silu(conv)
    q = qk[:, :A_QK]
    k = qk[:, A_QK:] * (A_DQK ** -0.5)

    for hd in range(A_HEADS):
        m_h = m_old[hd:hd + 1, 0:1]
        ct = ct_scr[hd]
        nrow = n_scr[hd:hd + 1, :]
        for c in range(T // L):
            rs = slice(c * L, (c + 1) * L)
            qh = q[rs, hd * A_DQK:(hd + 1) * A_DQK]
            kh = k[rs, hd * A_DQK:(hd + 1) * A_DQK]
            vh = v[rs, hd * A_DV:(hd + 1) * A_DV]
            qb = qh.astype(BF16)
            br = bb_row[hd:hd + 1, rs]
            t = jnp.where(causal, br, -jnp.inf)
            mc = jnp.maximum(jnp.max(t, axis=1, keepdims=True), m_h)
            m_last = jnp.maximum(jnp.max(br, axis=1, keepdims=True), m_h)
            b_last = b_col[(c + 1) * L - 1:(c + 1) * L, hd:hd + 1]

            w = jnp.exp(t - mc) * _dot_nt(qb, kh.astype(BF16))
            inter = jnp.exp(m_h - mc)
            num = _dot(w.astype(BF16), vh.astype(BF16)) + inter * _dot(qb, ct.astype(BF16))
            den = jnp.sum(w, axis=1, keepdims=True) + inter * jnp.sum(qh * nrow, axis=1, keepdims=True)
            floor = jnp.exp(-(b_col[rs, hd:hd + 1] + mc))
            hh_scr[rs, hd * A_DV:(hd + 1) * A_DV] = num * (1.0 / jnp.maximum(jnp.abs(den), floor))

            wj = jnp.exp(bb_col[rs, hd:hd + 1] - m_last)
            decay = jnp.exp(m_h - m_last)
            ct = decay * ct + _dot(kh.T.astype(BF16), (wj * vh).astype(BF16))
            nrow = decay * nrow + jnp.sum(wj * kh, axis=0, keepdims=True)
            m_h = b_last + m_last
        ct_scr[hd] = ct
        n_scr[hd:hd + 1, :] = nrow
        m_scr[hd:hd + 1, :] = jnp.broadcast_to(m_h, (1, LANES))

    hng = hng_ref[...]
    og = _sigmoid(o_pre)
    for hd in range(A_HEADS):
        sl = slice(hd * A_DV, (hd + 1) * A_DV)
        ht = og[:, sl] * hh_scr[:, sl]
        ynext_scr[:, sl] = (_rms(ht, hng[:, sl]) * _silu(z[:, sl])).astype(BF16)
    y_scr[...] = ynext_scr[...]


def _mlstm_layer(h, p, layer, norm_g, w_in, conv_w, b_gate, hnorm_g, w_out, ple_norm, w_pg, w_pp):
    B, S, D = h.shape
    L = MLSTM_TILE
    assert S % L == 0 and L % (2 * MLSTM_CHUNK) == 0
    P = p.shape[-1]
    wb = w_in.astype(BF16)
    o0 = 2 * A_QK
    wqk, wv, wo, wz = wb[:, :o0], wb[:, o0:o0 + A_VW], wb[:, o0 + A_VW:o0 + 2 * A_VW], wb[:, o0 + 2 * A_VW:o0 + 3 * A_VW]
    wgi = wb[:, o0 + 3 * A_VW:o0 + 3 * A_VW + A_HEADS]
    wgf = wb[:, o0 + 3 * A_VW + A_HEADS:]
    pad = jnp.zeros((D, LANES - A_HEADS), BF16)
    wg = jnp.concatenate([wgi, pad, wgf, pad], axis=1)
    bpad = jnp.zeros((LANES - A_HEADS,), F32)
    bg = jnp.concatenate([b_gate[:A_HEADS], bpad, b_gate[A_HEADS:], bpad])[None, :]

    per = S // L
    last = B * per - 1
    nxt = lambda t: (jnp.minimum(t, last) // per, jnp.minimum(t, last) % per, 0)
    prv = lambda t: (jnp.maximum(t - 1, 0) // per, jnp.maximum(t - 1, 0) % per, 0)
    in_specs = [
        pl.BlockSpec((None, L, D), nxt),
        pl.BlockSpec((None, L, D), prv),
        pl.BlockSpec((None, None, L, P), lambda t: (layer,) + prv(t)),
        _const_spec((1, D)),
        _const_spec((D, 2 * A_QK)), _const_spec((D, A_VW)), _const_spec((D, A_VW)), _const_spec((D, A_VW)),
        _const_spec((D, 2 * LANES)),
        _const_spec((A_CONV, 2 * A_QK)), _const_spec((1, 2 * LANES)), _const_spec((1, A_VW)),
        _const_spec((A_VW, D)), _const_spec((1, D)), _const_spec((D, D)), _const_spec((P, D)),
    ]
    return pl.pallas_call(
        functools.partial(_mlstm_body, tiles_per_seq=per),
        grid=(B * per + 1,),
        in_specs=in_specs,
        out_specs=pl.BlockSpec((None, L, D), prv),
        out_shape=jax.ShapeDtypeStruct((B, S, D), F32),
        scratch_shapes=[
            pltpu.VMEM((L + 8, 2 * A_QK), F32),
            pltpu.VMEM((L, A_VW), F32),
            pltpu.VMEM((A_HEADS, A_DQK, A_DV), F32),
            pltpu.VMEM((8, A_DQK), F32),
            pltpu.VMEM((8, LANES), F32),
            pltpu.VMEM((L, A_VW), BF16),
            pltpu.VMEM((L, A_VW), BF16),
        ],
        compiler_params=pltpu.CompilerParams(
            dimension_semantics=("arbitrary",), vmem_limit_bytes=VMEM_LIMIT),
        name="mlstm_layer",
    )(h, h, p, norm_g[None, :], wqk, wv, wo, wz, wg, conv_w, bg, hnorm_g[None, :],
      w_out.astype(BF16), ple_norm[None, :], w_pg.astype(BF16), w_pp.astype(BF16))


def _head_rms(x, g4, ones_bd):
    ss = _dot((x * x).astype(BF16), ones_bd)
    return x * lax.rsqrt(ss * (1.0 / B_HD) + EPS) * g4


def _rope(xn, cos2, sin2):
    return xn * cos2 + pltpu.roll(xn, B_HD // 2, axis=1) * sin2


def _qkv_body(*refs, with_kv):
    if with_kv:
        (h_ref, cos_ref, sin_ref, ones_ref, ng_ref, win_ref, qg_ref, nkv_ref, wkv_ref, kg_ref,
         q0_ref, q1_ref, q2_ref, z_ref, k0_ref, k1_ref, k2_ref, v0_ref, v1_ref, v2_ref, *slab) = refs
    else:
        (h_ref, cos_ref, sin_ref, ones_ref, ng_ref, win_ref, qg_ref, q0_ref, q1_ref, q2_ref, z_ref, *slab) = refs
    n = h_ref.shape[0]

    def deinterleave(val, slot, cs, dests):
        slab[slot][...] = val
        for ref, r in dests:
            for c in range(r):
                ref[c, :, cs] = slab[slot][pl.ds(c, n // r, stride=r), :].astype(BF16)

    h = h_ref[...]
    cos2 = cos_ref[...]
    sin2 = sin_ref[...]
    xn = _rms(h, ng_ref[...]).astype(BF16)
    qg = qg_ref[...]
    ones_bd = ones_ref[...]
    heads = [slice(hd * B_HD, (hd + 1) * B_HD) for hd in range(B_KV_HEADS)]

    pq = _head_rms(_dot(xn, win_ref[:, 2 * B_VW:3 * B_VW]), qg, ones_bd)
    for hd, cs in enumerate(heads):
        deinterleave(_rope(pq[:, cs], cos2, sin2), B_KV_HEADS + hd, cs, ((q2_ref, MAX_DIL),))
    if with_kv:
        xk = _rms(h, nkv_ref[...]).astype(BF16)
        pk = _head_rms(_dot(xk, wkv_ref[:, :B_VW]), kg_ref[...], ones_bd)
        for hd, cs in enumerate(heads):
            kh = _rope(pk[:, cs], cos2, sin2)
            k0_ref[:, cs] = kh.astype(BF16)
            deinterleave(kh, 2 * B_KV_HEADS + hd, cs, ((k1_ref, 4), (k2_ref, MAX_DIL)))
        pv = _dot(xk, wkv_ref[:, B_VW:])
        for hd, cs in enumerate(heads):
            v0_ref[:, cs] = pv[:, cs].astype(BF16)
            deinterleave(pv[:, cs], 3 * B_KV_HEADS + hd, cs, ((v1_ref, 4), (v2_ref, MAX_DIL)))
    pq = _head_rms(_dot(xn, win_ref[:, B_VW:2 * B_VW]), qg, ones_bd)
    for hd, cs in enumerate(heads):
        deinterleave(_rope(pq[:, cs], cos2, sin2), hd, cs, ((q1_ref, 4),))
    pq = _head_rms(_dot(xn, win_ref[:, :B_VW]), qg, ones_bd)
    for hd, cs in enumerate(heads):
        q0_ref[:, cs] = _rope(pq[:, cs], cos2, sin2).astype(BF16)
    z_ref[...] = _dot(xn, win_ref[:, 3 * B_VW:]).astype(BF16)


def _qkv_proj(h, cos2, sin2, norm_g, w_in, qnorm_g, kv_params=None):
    B, S, D = h.shape
    n = 1024
    assert S % n == 0 and n % (8 * MAX_DIL) == 0
    with_kv = kv_params is not None

    def lay_specs():
        shapes, specs = [], []
        for _, r in DILATED_GROUPS:
            shapes.append(jax.ShapeDtypeStruct((B, r, S // r, B_VW), BF16))
            if r == 1:
                specs.append(pl.BlockSpec((None, None, n, B_VW), lambda b, i: (b, 0, i, 0)))
            else:
                specs.append(pl.BlockSpec((None, r, n // r, B_VW), lambda b, i: (b, 0, i, 0)))
        return shapes, specs

    row = lambda b, i: (b, i, 0)
    out_shape, out_specs = lay_specs()
    out_shape.append(jax.ShapeDtypeStruct((B, S, B_VW), BF16))
    out_specs.append(pl.BlockSpec((None, n, B_VW), row))
    in_specs = [
        pl.BlockSpec((None, n, D), row),
        pl.BlockSpec((n, B_HD), lambda b, i: (i, 0)),
        pl.BlockSpec((n, B_HD), lambda b, i: (i, 0)),
        _const_spec((B_VW, B_VW)),
        _const_spec((1, D)), _const_spec((D, 4 * B_VW)), _const_spec((1, B_VW)),
    ]
    head_id = jnp.arange(B_VW) // B_HD
    ones_bd = (head_id[:, None] == head_id[None, :]).astype(BF16)
    tile4 = lambda g: jnp.tile(g, B_KV_HEADS)[None, :]
    args = [h, cos2, sin2, ones_bd, norm_g[None, :], w_in.astype(BF16), tile4(qnorm_g)]
    if with_kv:
        norm_kv, w_kv, knorm = kv_params
        in_specs += [_const_spec((1, D)), _const_spec((D, 2 * B_VW)), _const_spec((1, B_VW))]
        args += [norm_kv[None, :], w_kv.astype(BF16), tile4(knorm)]
        for _ in range(2):
            shp, sp = lay_specs()
            out_shape += shp
            out_specs += sp
    n_slots = (4 if with_kv else 2) * B_KV_HEADS
    outs = pl.pallas_call(
        functools.partial(_qkv_body, with_kv=with_kv),
        grid=(B, S // n),
        in_specs=in_specs,
        out_specs=out_specs,
        out_shape=out_shape,
        scratch_shapes=[pltpu.VMEM((n, LANES), F32) for _ in range(n_slots)],
        compiler_params=pltpu.CompilerParams(
            dimension_semantics=("arbitrary", "arbitrary"), vmem_limit_bytes=VMEM_LIMIT),
        name="qkv_proj_kv" if with_kv else "qkv_proj",
    )(*args)
    if with_kv:
        return outs[0:3], outs[3], outs[4:7], outs[7:10]
    return outs[0:3], outs[3]


def _attn_body(q_ref, kh_ref, kc_ref, vh_ref, vc_ref, o_ref, lse_ref, *scr, r):
    qblk = q_ref.shape[1]
    u0 = pl.program_id(1) * qblk
    nk = ATT_SUB + N_BACK
    qq = lax.broadcasted_iota(jnp.int32, (ATT_SUB, nk), 0)
    kk = lax.broadcasted_iota(jnp.int32, (ATT_SUB, nk), 1)
    dist = N_BACK + qq - kk
    band = (dist >= 0) & (dist <= N_BACK)
    lane = lax.broadcasted_iota(jnp.int32, (ATT_SUB, LANES), 1)
    scale = B_HD ** -0.5

    def residue(c):
        q = q_ref[c]
        k = jnp.concatenate([kh_ref[c], kc_ref[c]], axis=0)
        v = jnp.concatenate([vh_ref[c], vc_ref[c]], axis=0)
        for sb in range(qblk // ATT_SUB):
            r0 = sb * ATT_SUB
            kpos = u0 + (r0 - N_BACK) + kk
            bias = jnp.where(band & (kpos >= 0), 0.0, -jnp.inf)
            lse_blk = jnp.zeros((ATT_SUB, LANES), F32)
            for hd in range(B_KV_HEADS):
                cs = slice(hd * B_HD, (hd + 1) * B_HD)
                s = _dot_nt(q[r0:r0 + ATT_SUB, cs], k[r0:r0 + nk, cs]) * scale + bias
                mx = jnp.max(s, axis=1, keepdims=True)
                e = jnp.exp(s - mx)
                den = jnp.sum(e, axis=1, keepdims=True)
                o = _dot(e.astype(BF16), v[r0:r0 + nk, cs]) * (1.0 / den)
                lse_blk = jnp.where(lane == hd, mx + jnp.log(den), lse_blk)
                if r == 1:
                    o_ref[r0:r0 + ATT_SUB, cs] = o.astype(BF16)
                else:
                    scr[0][hd, pl.ds(r0 * r + c, ATT_SUB, stride=r), :] = o
            if r == 1:
                lse_ref[r0:r0 + ATT_SUB, :] = lse_blk
            else:
                scr[1][pl.ds(r0 * r + c, ATT_SUB, stride=r), :] = lse_blk

    if r <= 4:
        for c in range(r):
            residue(c)
    else:
        def loop_body(c, carry):
            residue(c)
            return carry
        lax.fori_loop(0, r, loop_body, 0, unroll=True)
    if r > 1:
        for hd in range(B_KV_HEADS):
            o_ref[:, hd * B_HD:(hd + 1) * B_HD] = scr[0][hd].astype(BF16)
        lse_ref[...] = scr[1][...]


def _window_attn(q, k, v):
    B, r, N, W = q.shape
    qblk = {1: 2048, 4: 512}.get(r, ATT_SUB)
    assert N % qblk == 0
    per = qblk // N_BACK
    tt = qblk * r
    cur = lambda b, i: (b, 0, i, 0)
    halo = lambda b, i: (b, 0, jnp.maximum(i * per - 1, 0), 0)
    row = lambda b, i: (b, i, 0)
    scratch = [] if r == 1 else [pltpu.VMEM((B_KV_HEADS, tt, LANES), F32), pltpu.VMEM((tt, LANES), F32)]
    return pl.pallas_call(
        functools.partial(_attn_body, r=r),
        grid=(B, N // qblk),
        in_specs=[
            pl.BlockSpec((None, r, qblk, W), cur),
            pl.BlockSpec((None, r, N_BACK, W), halo),
            pl.BlockSpec((None, r, qblk, W), cur),
            pl.BlockSpec((None, r, N_BACK, W), halo),
            pl.BlockSpec((None, r, qblk, W), cur),
        ],
        out_specs=[pl.BlockSpec((None, tt, W), row), pl.BlockSpec((None, tt, LANES), row)],
        out_shape=[
            jax.ShapeDtypeStruct((B, N * r, W), BF16),
            jax.ShapeDtypeStruct((B, N * r, LANES), F32),
        ],
        scratch_shapes=scratch,
        compiler_params=pltpu.CompilerParams(
            dimension_semantics=("arbitrary", "arbitrary"), vmem_limit_bytes=VMEM_LIMIT),
        name=f"window_attn_r{r}",
    )(q, k, k, v, v)


def _post_body(h_ref, p_ref, z_ref, o0_ref, o1_ref, o2_ref, l0_ref, l1_ref, l2_ref,
               wout_ref, png_ref, wpg_ref, wpp_ref, out_ref, y_scr, ynext_scr):
    @pl.when(pl.program_id(0) == 0)
    def _():
        y_scr[...] = jnp.zeros(y_scr.shape, BF16)

    half = h_ref.shape[0] // 2
    halves = [slice(r0, r0 + half) for r0 in (0, half)]
    hns = [h_ref[rs, :] + _dot(y_scr[rs, :], wout_ref[...]) for rs in halves]
    for rs, hn in zip(halves, hns):
        gate = _sigmoid(_dot(_rms(hn, png_ref[...]).astype(BF16), wpg_ref[...]))
        emb = _dot(p_ref[rs, :].astype(BF16), wpp_ref[...])
        out_ref[rs, :] = hn + emb * gate

    l0, l1, l2 = l0_ref[...], l1_ref[...], l2_ref[...]
    mx = jnp.maximum(jnp.maximum(l0, l1), l2)
    e0, e1, e2 = jnp.exp(l0 - mx), jnp.exp(l1 - mx), jnp.exp(l2 - mx)
    tot = e0 + e1 + e2
    w0, w1, w2 = e0 / tot, e1 / tot, e2 / tot
    for hd in range(B_KV_HEADS):
        cs = slice(hd * B_HD, (hd + 1) * B_HD)
        o = (w0[:, hd:hd + 1] * o0_ref[:, cs].astype(F32) + w1[:, hd:hd + 1] * o1_ref[:, cs].astype(F32)
             + w2[:, hd:hd + 1] * o2_ref[:, cs].astype(F32))
        ynext_scr[:, cs] = (o * _silu(z_ref[:, cs].astype(F32))).astype(BF16)
    y_scr[...] = ynext_scr[...]


def _attn_post(h, p, layer, z, os_, lses, w_out, ple_norm, w_pg, w_pp):
    B, S, D = h.shape
    P = p.shape[-1]
    n = 1024
    assert S % n == 0
    per = S // n
    last = B * per - 1
    nxt = lambda t: (jnp.minimum(t, last) // per, jnp.minimum(t, last) % per, 0)
    prv = lambda t: (jnp.maximum(t - 1, 0) // per, jnp.maximum(t - 1, 0) % per, 0)
    in_specs = [
        pl.BlockSpec((None, n, D), prv),
        pl.BlockSpec((None, None, n, P), lambda t: (layer,) + prv(t)),
        pl.BlockSpec((None, n, B_VW), nxt),
        pl.BlockSpec((None, n, B_VW), nxt), pl.BlockSpec((None, n, B_VW), nxt), pl.BlockSpec((None, n, B_VW), nxt),
        pl.BlockSpec((None, n, LANES), nxt), pl.BlockSpec((None, n, LANES), nxt), pl.BlockSpec((None, n, LANES), nxt),
        _const_spec((B_VW, D)), _const_spec((1, D)), _const_spec((D, D)), _const_spec((P, D)),
    ]
    return pl.pallas_call(
        _post_body,
        grid=(B * per + 1,),
        in_specs=in_specs,
        out_specs=pl.BlockSpec((None, n, D), prv),
        out_shape=jax.ShapeDtypeStruct((B, S, D), F32),
        scratch_shapes=[pltpu.VMEM((n, B_VW), BF16), pltpu.VMEM((n, B_VW), BF16)],
        compiler_params=pltpu.CompilerParams(
            dimension_semantics=("arbitrary",), vmem_limit_bytes=VMEM_LIMIT),
        name="attn_post",
    )(h, p, z, *os_, *lses, w_out.astype(BF16), ple_norm[None, :], w_pg.astype(BF16), w_pp.astype(BF16))


def _rope_tables(seq):
    inv = ROPE_THETA ** (-jnp.arange(0, B_HD, 2, dtype=F32) / B_HD)
    ang = jnp.arange(seq, dtype=F32)[:, None] * inv[None, :]
    cos, sin = jnp.cos(ang), jnp.sin(ang)
    return jnp.concatenate([cos, cos], axis=1), jnp.concatenate([-sin, sin], axis=1)


def kernel(x, p, norm_a, w_in_a, conv_a, b_gate_a, hnorm_a, w_out_a, norm_kv, w_kv, knorm, norm_b, w_in_b, qnorm_b, w_out_b, ple_norm, w_ple_gate, w_ple):
    S = x.shape[1]
    n_a = norm_a.shape[0]
    n_b = norm_b.shape[0]
    assert all(w // d == N_BACK for w, d in DILATED_GROUPS)
    cos2, sin2 = _rope_tables(S)
    h = x
    for l in range(n_a):
        h = _mlstm_layer(h, p, l, norm_a[l], w_in_a[l], conv_a[l], b_gate_a[l], hnorm_a[l], w_out_a[l],
                         ple_norm[l], w_ple_gate[l], w_ple[l])
    ks = vs = None
    for j in range(n_b):
        l = n_a + j
        if j == 0:
            qs, z, ks, vs = _qkv_proj(h, cos2, sin2, norm_b[j], w_in_b[j], qnorm_b[j], (norm_kv, w_kv, knorm))
        else:
            qs, z = _qkv_proj(h, cos2, sin2, norm_b[j], w_in_b[j], qnorm_b[j])
        res = [_window_attn(qs[g], ks[g], vs[g]) for g in range(len(DILATED_GROUPS))]
        h = _attn_post(h, p, l, z, [r[0] for r in res], [r[1] for r in res],
                       w_out_b[j], ple_norm[l], w_ple_gate[l], w_ple[l])
    return h
```

```python
import functools

import jax
import jax.numpy as jnp
from jax import lax
from jax.experimental import pallas as pl
from jax.experimental.pallas import tpu as pltpu

F32 = jnp.float32
BF16 = jnp.bfloat16

EPS = 1e-6
ROPE_THETA = 10000.0

A_HEADS = 4
A_DQK = 128
A_DV = 256
A_QK = A_HEADS * A_DQK
A_VW = A_HEADS * A_DV
A_CONV = 4
MLSTM_CHUNK = 128
MLSTM_TILE = 512

B_KV_HEADS = 4
B_HD = 128
B_VW = B_KV_HEADS * B_HD
DILATED_GROUPS = ((128, 1), (512, 4), (2048, 16))
N_BACK = 128
MAX_DIL = 16
ATT_SUB = 128

LANES = 128
VMEM_LIMIT = 56 * 1024 * 1024


def _dot(a, b):
    return jnp.dot(a, b, preferred_element_type=F32)


def _dot_nt(a, b):
    return lax.dot_general(a, b, (((1,), (1,)), ((), ())), preferred_element_type=F32)


def _rms(x, g):
    ms = jnp.mean(x * x, axis=-1, keepdims=True)
    return x * lax.rsqrt(ms + EPS) * g


def _sigmoid(x):
    return 1.0 / (1.0 + jnp.exp(-x))


def _silu(x):
    return x * _sigmoid(x)


def _log_sigmoid(x):
    return jnp.minimum(x, 0.0) - jnp.log1p(jnp.exp(-jnp.abs(x)))


def _const_spec(shape):
    nd = len(shape)
    return pl.BlockSpec(shape, lambda *_: (0,) * nd, pipeline_mode=pl.Buffered(1))


def _mlstm_body(h_ref, hprev_ref, pprev_ref, ng_ref, wqk_ref, wv_ref, wo_ref, wz_ref, wg_ref, conv_ref, bg_ref,
                hng_ref, wout_ref, png_ref, wpg_ref, wpp_ref, out_ref,
                qk_scr, hh_scr, ct_scr, n_scr, m_scr, y_scr, ynext_scr, *, tiles_per_seq):
    T = h_ref.shape[0]
    L = MLSTM_CHUNK
    t = pl.program_id(0)

    @pl.when(t == 0)
    def _():
        y_scr[...] = jnp.zeros(y_scr.shape, BF16)

    @pl.when(t % tiles_per_seq == 0)
    def _():
        qk_scr[0:8, :] = jnp.zeros((8, 2 * A_QK), F32)
        ct_scr[...] = jnp.zeros(ct_scr.shape, F32)
        n_scr[...] = jnp.zeros(n_scr.shape, F32)
        m_scr[...] = jnp.zeros(m_scr.shape, F32)

    half = T // 2
    halves = [slice(r0, r0 + half) for r0 in (0, half)]
    hns = [hprev_ref[rs, :] + _dot(y_scr[rs, :], wout_ref[...]) for rs in halves]
    for rs, hn in zip(halves, hns):
        gate = _sigmoid(_dot(_rms(hn, png_ref[...]).astype(BF16), wpg_ref[...]))
        emb = _dot(pprev_ref[rs, :].astype(BF16), wpp_ref[...])
        out_ref[rs, :] = hn + emb * gate

    h = h_ref[...]
    xn = _rms(h, ng_ref[...]).astype(BF16)
    g2 = _dot(xn, wg_ref[...]) + bg_ref[...]
    gi = g2[:, :LANES]
    gf = g2[:, LANES:]

    row_i = lax.broadcasted_iota(jnp.int32, (L, L), 0)
    col_i = lax.broadcasted_iota(jnp.int32, (L, L), 1)
    causal = col_i <= row_i
    tril = jnp.where(causal, 1.0, 0.0).astype(BF16)
    lf = _log_sigmoid(gf)
    lf1 = lf.astype(BF16)
    r1 = lf - lf1.astype(F32)
    lf2 = r1.astype(BF16)
    lf3 = (r1 - lf2.astype(F32)).astype(BF16)
    b_col = jnp.concatenate(
        [_dot(tril, lf1[c * L:(c + 1) * L]) + _dot(tril, lf2[c * L:(c + 1) * L]) + _dot(tril, lf3[c * L:(c + 1) * L])
         for c in range(T // L)], axis=0)
    bb_col = gi - b_col
    bb_row = bb_col.T[0:8, :]
    m_old = m_scr[...]

    qk_pre = _dot(xn, wqk_ref[...])
    v = _dot(xn, wv_ref[...])
    o_pre = _dot(xn, wo_ref[...])
    z = _dot(xn, wz_ref[...])

    qk_scr[8:8 + T, :] = qk_pre
    cw = conv_ref[...]
    conv = (qk_scr[5:5 + T, :] * cw[0:1, :] + qk_scr[6:6 + T, :] * cw[1:2, :]
            + qk_scr[7:7 + T, :] * cw[2:3, :] + qk_pre * cw[3:4, :])
    qk_scr[0:8, :] = qk_scr[T:T + 8, :]
    qk = _silu(conv)
    q = qk[:, :A_QK]
    k = qk[:, A_QK:] * (A_DQK ** -0.5)

    for hd in range(A_HEADS):
        m_h = m_old[hd:hd + 1, 0:1]
        ct = ct_scr[hd]
        nrow = n_scr[hd:hd + 1, :]
        for c in range(T // L):
            rs = slice(c * L, (c + 1) * L)
            qh = q[rs, hd * A_DQK:(hd + 1) * A_DQK]
            kh = k[rs, hd * A_DQK:(hd + 1) * A_DQK]
            vh = v[rs, hd * A_DV:(hd + 1) * A_DV]
            qb = qh.astype(BF16)
            br = bb_row[hd:hd + 1, rs]
            t = jnp.where(causal, br, -jnp.inf)
            mc = jnp.maximum(jnp.max(t, axis=1, keepdims=True), m_h)
            m_last = jnp.maximum(jnp.max(br, axis=1, keepdims=True), m_h)
            b_last = b_col[(c + 1) * L - 1:(c + 1) * L, hd:hd + 1]

            sc = _dot_nt(qb, kh.astype(BF16))
            r2 = _dot(qb, ct.astype(BF16))
            wj = jnp.exp(bb_col[rs, hd:hd + 1] - m_last)
            upd = _dot(kh.T.astype(BF16), (wj * vh).astype(BF16))
            w = jnp.exp(t - mc) * sc
            inter = jnp.exp(m_h - mc)
            num = _dot(w.astype(BF16), vh.astype(BF16)) + inter * r2
            den = jnp.sum(w, axis=1, keepdims=True) + inter * jnp.sum(qh * nrow, axis=1, keepdims=True)
            floor = jnp.exp(-(b_col[rs, hd:hd + 1] + mc))
            hh_scr[rs, hd * A_DV:(hd + 1) * A_DV] = num * (1.0 / jnp.maximum(jnp.abs(den), floor))

            decay = jnp.exp(m_h - m_last)
            ct = decay * ct + upd
            nrow = decay * nrow + jnp.sum(wj * kh, axis=0, keepdims=True)
            m_h = b_last + m_last
        ct_scr[hd] = ct
        n_scr[hd:hd + 1, :] = nrow
        m_scr[hd:hd + 1, :] = jnp.broadcast_to(m_h, (1, LANES))

    hng = hng_ref[...]
    og = _sigmoid(o_pre)
    for hd in range(A_HEADS):
        sl = slice(hd * A_DV, (hd + 1) * A_DV)
        ht = og[:, sl] * hh_scr[:, sl]
        ynext_scr[:, sl] = (_rms(ht, hng[:, sl]) * _silu(z[:, sl])).astype(BF16)
    y_scr[...] = ynext_scr[...]


def _mlstm_layer(h, p, layer, norm_g, w_in, conv_w, b_gate, hnorm_g, w_out, ple_norm, w_pg, w_pp):
    B, S, D = h.shape
    L = MLSTM_TILE
    assert S % L == 0 and L % (2 * MLSTM_CHUNK) == 0
    P = p.shape[-1]
    wb = w_in.astype(BF16)
    o0 = 2 * A_QK
    wqk, wv, wo, wz = wb[:, :o0], wb[:, o0:o0 + A_VW], wb[:, o0 + A_VW:o0 + 2 * A_VW], wb[:, o0 + 2 * A_VW:o0 + 3 * A_VW]
    wgi = wb[:, o0 + 3 * A_VW:o0 + 3 * A_VW + A_HEADS]
    wgf = wb[:, o0 + 3 * A_VW + A_HEADS:]
    pad = jnp.zeros((D, LANES - A_HEADS), BF16)
    wg = jnp.concatenate([wgi, pad, wgf, pad], axis=1)
    bpad = jnp.zeros((LANES - A_HEADS,), F32)
    bg = jnp.concatenate([b_gate[:A_HEADS], bpad, b_gate[A_HEADS:], bpad])[None, :]

    per = S // L
    last = B * per - 1
    nxt = lambda t: (jnp.minimum(t, last) // per, jnp.minimum(t, last) % per, 0)
    prv = lambda t: (jnp.maximum(t - 1, 0) // per, jnp.maximum(t - 1, 0) % per, 0)
    in_specs = [
        pl.BlockSpec((None, L, D), nxt),
        pl.BlockSpec((None, L, D), prv),
        pl.BlockSpec((None, None, L, P), lambda t: (layer,) + prv(t)),
        _const_spec((1, D)),
        _const_spec((D, 2 * A_QK)), _const_spec((D, A_VW)), _const_spec((D, A_VW)), _const_spec((D, A_VW)),
        _const_spec((D, 2 * LANES)),
        _const_spec((A_CONV, 2 * A_QK)), _const_spec((1, 2 * LANES)), _const_spec((1, A_VW)),
        _const_spec((A_VW, D)), _const_spec((1, D)), _const_spec((D, D)), _const_spec((P, D)),
    ]
    return pl.pallas_call(
        functools.partial(_mlstm_body, tiles_per_seq=per),
        grid=(B * per + 1,),
        in_specs=in_specs,
        out_specs=pl.BlockSpec((None, L, D), prv),
        out_shape=jax.ShapeDtypeStruct((B, S, D), F32),
        scratch_shapes=[
            pltpu.VMEM((L + 8, 2 * A_QK), F32),
            pltpu.VMEM((L, A_VW), F32),
            pltpu.VMEM((A_HEADS, A_DQK, A_DV), F32),
            pltpu.VMEM((8, A_DQK), F32),
            pltpu.VMEM((8, LANES), F32),
            pltpu.VMEM((L, A_VW), BF16),
            pltpu.VMEM((L, A_VW), BF16),
        ],
        compiler_params=pltpu.CompilerParams(
            dimension_semantics=("arbitrary",), vmem_limit_bytes=VMEM_LIMIT),
        name="mlstm_layer",
    )(h, h, p, norm_g[None, :], wqk, wv, wo, wz, wg, conv_w, bg, hnorm_g[None, :],
      w_out.astype(BF16), ple_norm[None, :], w_pg.astype(BF16), w_pp.astype(BF16))


def _head_rms(x, g4, ones_bd):
    ss = _dot((x * x).astype(BF16), ones_bd)
    return x * lax.rsqrt(ss * (1.0 / B_HD) + EPS) * g4


def _rope(xn, cos2, sin2):
    return xn * cos2 + pltpu.roll(xn, B_HD // 2, axis=1) * sin2


def _qkv_body(*refs, with_kv):
    if with_kv:
        (h_ref, cos_ref, sin_ref, ones_ref, ng_ref, win_ref, qg_ref, nkv_ref, wkv_ref, kg_ref,
         q0_ref, q1_ref, q2_ref, z_ref, k0_ref, k1_ref, k2_ref, v0_ref, v1_ref, v2_ref, *slab) = refs
    else:
        (h_ref, cos_ref, sin_ref, ones_ref, ng_ref, win_ref, qg_ref, q0_ref, q1_ref, q2_ref, z_ref, *slab) = refs
    n = h_ref.shape[0]

    def deinterleave(val, slot, cs, dests):
        slab[slot][...] = val
        for ref, r in dests:
            for c in range(r):
                ref[c, :, cs] = slab[slot][pl.ds(c, n // r, stride=r), :].astype(BF16)

    h = h_ref[...]
    cos2 = cos_ref[...]
    sin2 = sin_ref[...]
    xn = _rms(h, ng_ref[...]).astype(BF16)
    qg = qg_ref[...]
    ones_bd = ones_ref[...]
    heads = [slice(hd * B_HD, (hd + 1) * B_HD) for hd in range(B_KV_HEADS)]

    pq = _head_rms(_dot(xn, win_ref[:, 2 * B_VW:3 * B_VW]), qg, ones_bd)
    for hd, cs in enumerate(heads):
        deinterleave(_rope(pq[:, cs], cos2, sin2), B_KV_HEADS + hd, cs, ((q2_ref, MAX_DIL),))
    if with_kv:
        xk = _rms(h, nkv_ref[...]).astype(BF16)
        pk = _head_rms(_dot(xk, wkv_ref[:, :B_VW]), kg_ref[...], ones_bd)
        for hd, cs in enumerate(heads):
            kh = _rope(pk[:, cs], cos2, sin2)
            k0_ref[:, cs] = kh.astype(BF16)
            deinterleave(kh, 2 * B_KV_HEADS + hd, cs, ((k1_ref, 4), (k2_ref, MAX_DIL)))
        pv = _dot(xk, wkv_ref[:, B_VW:])
        for hd, cs in enumerate(heads):
            v0_ref[:, cs] = pv[:, cs].astype(BF16)
            deinterleave(pv[:, cs], 3 * B_KV_HEADS + hd, cs, ((v1_ref, 4), (v2_ref, MAX_DIL)))
    pq = _head_rms(_dot(xn, win_ref[:, B_VW:2 * B_VW]), qg, ones_bd)
    for hd, cs in enumerate(heads):
        deinterleave(_rope(pq[:, cs], cos2, sin2), hd, cs, ((q1_ref, 4),))
    pq = _head_rms(_dot(xn, win_ref[:, :B_VW]), qg, ones_bd)
    for hd, cs in enumerate(heads):
        q0_ref[:, cs] = _rope(pq[:, cs], cos2, sin2).astype(BF16)
    z_ref[...] = _dot(xn, win_ref[:, 3 * B_VW:]).astype(BF16)


def _qkv_proj(h, cos2, sin2, norm_g, w_in, qnorm_g, kv_params=None):
    B, S, D = h.shape
    n = 1024
    assert S % n == 0 and n % (8 * MAX_DIL) == 0
    with_kv = kv_params is not None

    def lay_specs():
        shapes, specs = [], []
        for _, r in DILATED_GROUPS:
            shapes.append(jax.ShapeDtypeStruct((B, r, S // r, B_VW), BF16))
            if r == 1:
                specs.append(pl.BlockSpec((None, None, n, B_VW), lambda b, i: (b, 0, i, 0)))
            else:
                specs.append(pl.BlockSpec((None, r, n // r, B_VW), lambda b, i: (b, 0, i, 0)))
        return shapes, specs

    row = lambda b, i: (b, i, 0)
    out_shape, out_specs = lay_specs()
    out_shape.append(jax.ShapeDtypeStruct((B, S, B_VW), BF16))
    out_specs.append(pl.BlockSpec((None, n, B_VW), row))
    in_specs = [
        pl.BlockSpec((None, n, D), row),
        pl.BlockSpec((n, B_HD), lambda b, i: (i, 0)),
        pl.BlockSpec((n, B_HD), lambda b, i: (i, 0)),
        _const_spec((B_VW, B_VW)),
        _const_spec((1, D)), _const_spec((D, 4 * B_VW)), _const_spec((1, B_VW)),
    ]
    head_id = jnp.arange(B_VW) // B_HD
    ones_bd = (head_id[:, None] == head_id[None, :]).astype(BF16)
    tile4 = lambda g: jnp.tile(g, B_KV_HEADS)[None, :]
    args = [h, cos2, sin2, ones_bd, norm_g[None, :], w_in.astype(BF16), tile4(qnorm_g)]
    if with_kv:
        norm_kv, w_kv, knorm = kv_params
        in_specs += [_const_spec((1, D)), _const_spec((D, 2 * B_VW)), _const_spec((1, B_VW))]
        args += [norm_kv[None, :], w_kv.astype(BF16), tile4(knorm)]
        for _ in range(2):
            shp, sp = lay_specs()
            out_shape += shp
            out_specs += sp
    n_slots = (4 if with_kv else 2) * B_KV_HEADS
    outs = pl.pallas_call(
        functools.partial(_qkv_body, with_kv=with_kv),
        grid=(B, S // n),
        in_specs=in_specs,
        out_specs=out_specs,
        out_shape=out_shape,
        scratch_shapes=[pltpu.VMEM((n, LANES), F32) for _ in range(n_slots)],
        compiler_params=pltpu.CompilerParams(
            dimension_semantics=("arbitrary", "arbitrary"), vmem_limit_bytes=VMEM_LIMIT),
        name="qkv_proj_kv" if with_kv else "qkv_proj",
    )(*args)
    if with_kv:
        return outs[0:3], outs[3], outs[4:7], outs[7:10]
    return outs[0:3], outs[3]


def _attn_body(q_ref, kh_ref, kc_ref, vh_ref, vc_ref, o_ref, lse_ref, *scr, r):
    qblk = q_ref.shape[1]
    u0 = pl.program_id(1) * qblk
    nk = ATT_SUB + N_BACK
    qq = lax.broadcasted_iota(jnp.int32, (ATT_SUB, nk), 0)
    kk = lax.broadcasted_iota(jnp.int32, (ATT_SUB, nk), 1)
    dist = N_BACK + qq - kk
    band = (dist >= 0) & (dist <= N_BACK)
    lane = lax.broadcasted_iota(jnp.int32, (ATT_SUB, LANES), 1)
    scale = B_HD ** -0.5

    heads = [slice(hd * B_HD, (hd + 1) * B_HD) for hd in range(B_KV_HEADS)]
    blocks = [(c, sb) for c in range(r) for sb in range(qblk // ATT_SUB)]

    def scores(c, sb):
        r0 = sb * ATT_SUB
        kwin = (kh_ref[c], kc_ref[c, 0:ATT_SUB, :]) if sb == 0 else (kc_ref[c, r0 - N_BACK:r0 + ATT_SUB, :],)
        kw = jnp.concatenate(kwin, axis=0) if sb == 0 else kwin[0]
        return [_dot_nt(q_ref[c, r0:r0 + ATT_SUB, cs], kw[:, cs]) for cs in heads]

    sc_next = scores(*blocks[0])
    for bi, (c, sb) in enumerate(blocks):
        sc = sc_next
        if bi + 1 < len(blocks):
            sc_next = scores(*blocks[bi + 1])
        r0 = sb * ATT_SUB
        if sb == 0:
            vw = jnp.concatenate([vh_ref[c], vc_ref[c, 0:ATT_SUB, :]], axis=0)
        else:
            vw = vc_ref[c, r0 - N_BACK:r0 + ATT_SUB, :]
        kpos = u0 + (r0 - N_BACK) + kk
        bias = jnp.where(band & (kpos >= 0), 0.0, -jnp.inf)
        lse_blk = jnp.zeros((ATT_SUB, LANES), F32)
        for hd, cs in enumerate(heads):
            s = sc[hd] * scale + bias
            mx = jnp.max(s, axis=1, keepdims=True)
            e = jnp.exp(s - mx)
            den = jnp.sum(e, axis=1, keepdims=True)
            o = _dot(e.astype(BF16), vw[:, cs]) * (1.0 / den)
            lse_blk = jnp.where(lane == hd, mx + jnp.log(den), lse_blk)
            if r == 1:
                o_ref[r0:r0 + ATT_SUB, cs] = o.astype(BF16)
            else:
                scr[0][hd, pl.ds(r0 * r + c, ATT_SUB, stride=r), :] = o
        if r == 1:
            lse_ref[r0:r0 + ATT_SUB, :] = lse_blk
        else:
            scr[1][pl.ds(r0 * r + c, ATT_SUB, stride=r), :] = lse_blk

    if r > 1:
        for hd in range(B_KV_HEADS):
            o_ref[:, hd * B_HD:(hd + 1) * B_HD] = scr[0][hd].astype(BF16)
        lse_ref[...] = scr[1][...]


def _window_attn(q, k, v):
    B, r, N, W = q.shape
    qblk = {1: 2048, 4: 512}.get(r, ATT_SUB)
    assert N % qblk == 0
    per = qblk // N_BACK
    tt = qblk * r
    cur = lambda b, i: (b, 0, i, 0)
    halo = lambda b, i: (b, 0, jnp.maximum(i * per - 1, 0), 0)
    row = lambda b, i: (b, i, 0)
    scratch = [] if r == 1 else [pltpu.VMEM((B_KV_HEADS, tt, LANES), F32), pltpu.VMEM((tt, LANES), F32)]
    return pl.pallas_call(
        functools.partial(_attn_body, r=r),
        grid=(B, N // qblk),
        in_specs=[
            pl.BlockSpec((None, r, qblk, W), cur),
            pl.BlockSpec((None, r, N_BACK, W), halo),
            pl.BlockSpec((None, r, qblk, W), cur),
            pl.BlockSpec((None, r, N_BACK, W), halo),
            pl.BlockSpec((None, r, qblk, W), cur),
        ],
        out_specs=[pl.BlockSpec((None, tt, W), row), pl.BlockSpec((None, tt, LANES), row)],
        out_shape=[
            jax.ShapeDtypeStruct((B, N * r, W), BF16),
            jax.ShapeDtypeStruct((B, N * r, LANES), F32),
        ],
        scratch_shapes=scratch,
        compiler_params=pltpu.CompilerParams(
            dimension_semantics=("arbitrary", "arbitrary"), vmem_limit_bytes=VMEM_LIMIT),
        name=f"window_attn_r{r}",
    )(q, k, k, v, v)


def _post_body(h_ref, p_ref, z_ref, o0_ref, o1_ref, o2_ref, l0_ref, l1_ref, l2_ref,
               wout_ref, png_ref, wpg_ref, wpp_ref, out_ref, y_scr, ynext_scr):
    @pl.when(pl.program_id(0) == 0)
    def _():
        y_scr[...] = jnp.zeros(y_scr.shape, BF16)

    half = h_ref.shape[0] // 2
    halves = [slice(r0, r0 + half) for r0 in (0, half)]
    hns = [h_ref[rs, :] + _dot(y_scr[rs, :], wout_ref[...]) for rs in halves]
    for rs, hn in zip(halves, hns):
        gate = _sigmoid(_dot(_rms(hn, png_ref[...]).astype(BF16), wpg_ref[...]))
        emb = _dot(p_ref[rs, :].astype(BF16), wpp_ref[...])
        out_ref[rs, :] = hn + emb * gate

    l0, l1, l2 = l0_ref[...], l1_ref[...], l2_ref[...]
    mx = jnp.maximum(jnp.maximum(l0, l1), l2)
    e0, e1, e2 = jnp.exp(l0 - mx), jnp.exp(l1 - mx), jnp.exp(l2 - mx)
    tot = e0 + e1 + e2
    w0, w1, w2 = e0 / tot, e1 / tot, e2 / tot
    for hd in range(B_KV_HEADS):
        cs = slice(hd * B_HD, (hd + 1) * B_HD)
        o = (w0[:, hd:hd + 1] * o0_ref[:, cs].astype(F32) + w1[:, hd:hd + 1] * o1_ref[:, cs].astype(F32)
             + w2[:, hd:hd + 1] * o2_ref[:, cs].astype(F32))
        ynext_scr[:, cs] = (o * _silu(z_ref[:, cs].astype(F32))).astype(BF16)
    y_scr[...] = ynext_scr[...]


def _attn_post(h, p, layer, z, os_, lses, w_out, ple_norm, w_pg, w_pp):
    B, S, D = h.shape
    P = p.shape[-1]
    n = 1024
    assert S % n == 0
    per = S // n
    last = B * per - 1
    nxt = lambda t: (jnp.minimum(t, last) // per, jnp.minimum(t, last) % per, 0)
    prv = lambda t: (jnp.maximum(t - 1, 0) // per, jnp.maximum(t - 1, 0) % per, 0)
    in_specs = [
        pl.BlockSpec((None, n, D), prv),
        pl.BlockSpec((None, None, n, P), lambda t: (layer,) + prv(t)),
        pl.BlockSpec((None, n, B_VW), nxt),
        pl.BlockSpec((None, n, B_VW), nxt), pl.BlockSpec((None, n, B_VW), nxt), pl.BlockSpec((None, n, B_VW), nxt),
        pl.BlockSpec((None, n, LANES), nxt), pl.BlockSpec((None, n, LANES), nxt), pl.BlockSpec((None, n, LANES), nxt),
        _const_spec((B_VW, D)), _const_spec((1, D)), _const_spec((D, D)), _const_spec((P, D)),
    ]
    return pl.pallas_call(
        _post_body,
        grid=(B * per + 1,),
        in_specs=in_specs,
        out_specs=pl.BlockSpec((None, n, D), prv),
        out_shape=jax.ShapeDtypeStruct((B, S, D), F32),
        scratch_shapes=[pltpu.VMEM((n, B_VW), BF16), pltpu.VMEM((n, B_VW), BF16)],
        compiler_params=pltpu.CompilerParams(
            dimension_semantics=("arbitrary",), vmem_limit_bytes=VMEM_LIMIT),
        name="attn_post",
    )(h, p, z, *os_, *lses, w_out.astype(BF16), ple_norm[None, :], w_pg.astype(BF16), w_pp.astype(BF16))


def _rope_tables(seq):
    inv = ROPE_THETA ** (-jnp.arange(0, B_HD, 2, dtype=F32) / B_HD)
    ang = jnp.arange(seq, dtype=F32)[:, None] * inv[None, :]
    cos, sin = jnp.cos(ang), jnp.sin(ang)
    return jnp.concatenate([cos, cos], axis=1), jnp.concatenate([-sin, sin], axis=1)


def kernel(x, p, norm_a, w_in_a, conv_a, b_gate_a, hnorm_a, w_out_a, norm_kv, w_kv, knorm, norm_b, w_in_b, qnorm_b, w_out_b, ple_norm, w_ple_gate, w_ple):
    S = x.shape[1]
    n_a = norm_a.shape[0]
    n_b = norm_b.shape[0]
    assert all(w // d == N_BACK for w, d in DILATED_GROUPS)
    cos2, sin2 = _rope_tables(S)
    h = x
    for l in range(n_a):
        h = _mlstm_layer(h, p, l, norm_a[l], w_in_a[l], conv_a[l], b_gate_a[l], hnorm_a[l], w_out_a[l],
                         ple_norm[l], w_ple_gate[l], w_ple[l])
    ks = vs = None
    for j in range(n_b):
        l = n_a + j
        if j == 0:
            qs, z, ks, vs = _qkv_proj(h, cos2, sin2, norm_b[j], w_in_b[j], qnorm_b[j], (norm_kv, w_kv, knorm))
        else:
            qs, z = _qkv_proj(h, cos2, sin2, norm_b[j], w_in_b[j], qnorm_b[j])
        res = [_window_attn(qs[g], ks[g], vs[g]) for g in range(len(DILATED_GROUPS))]
        h = _attn_post(h, p, l, z, [r[0] for r in res], [r[1] for r in res],
                       w_out_b[j], ple_norm[l], w_ple_gate[l], w_ple[l])
    return h
```
